```python
import math
import jax, jax.numpy as jnp
from jax import lax
import numpy as np

D_MODEL = 2048
BATCH = 2
SEQ = 16384
DEPTH = 2

CHUNK = 64
Q_BLOCK = 128
HEAD_DIM = 64
D_MIX = D_MODEL
N_GROUPS = 4
GROUP_W = D_MIX // N_GROUPS
SB_HEADS = GROUP_W // HEAD_DIM
SB_WINDOW = 1024
DIFF_HEADS = GROUP_W // (2 * HEAD_DIM)
DIFF_WINDOW = 1024
LRU_BLOCKS = 8
LRU_BW = GROUP_W // LRU_BLOCKS
LRU_C = 8.0
LRU_CONV = 4
SC_CONV = 3
D_FF = 2 * D_MODEL
FFN_CONV = 3
N_BUCKETS = 32
MAX_DISTANCE = 128
N_IN_BLOCKS = 11
IN_COLS = N_IN_BLOCKS * GROUP_W
NORM_EPS = 1e-6
NEG_INF = -1e30

kernel_name = "hybrid_parallel_group_stream_encoder"


def rms_norm(x, g):
    xf = x.astype(jnp.float32)
    y = xf * lax.rsqrt(jnp.mean(xf * xf, axis=-1, keepdims=True) + NORM_EPS)
    return (y * g.astype(jnp.float32)).astype(x.dtype)


def causal_dwconv(x, w):
    K = w.shape[0]
    S = x.shape[1]
    xp = jnp.pad(x, ((0, 0), (K - 1, 0), (0, 0)))
    y = xp[:, 0:S] * w[0]
    for k in range(1, K):
        y = y + xp[:, k:k + S] * w[k]
    return y


def t5_bucket(rel):
    nb = N_BUCKETS // 2
    max_exact = nb // 2
    ret = jnp.where(rel > 0, nb, 0)
    n = jnp.abs(rel)
    large = max_exact + (jnp.log(jnp.maximum(n, 1).astype(jnp.float32) / max_exact)
                         / math.log(MAX_DISTANCE / max_exact) * (nb - max_exact)).astype(jnp.int32)
    large = jnp.minimum(large, nb - 1)
    return ret + jnp.where(n < max_exact, n, large)


def stick_breaking_attention(q, k, v):
    Bsz, H, S, d = q.shape
    nb = S // Q_BLOCK
    span = SB_WINDOW + Q_BLOCK
    nkb = span // Q_BLOCK
    scale = d ** -0.5
    kp = jnp.pad(k, ((0, 0), (0, 0), (SB_WINDOW, 0), (0, 0)))
    vp = jnp.pad(v, ((0, 0), (0, 0), (SB_WINDOW, 0), (0, 0)))
    rel = (jnp.arange(span) - SB_WINDOW)[None, :] - jnp.arange(Q_BLOCK)[:, None]
    band = (rel < 0) & (rel >= -SB_WINDOW)
    tri = (jnp.arange(Q_BLOCK)[:, None] > jnp.arange(Q_BLOCK)[None, :]).astype(jnp.float32)
    qb = q.reshape(Bsz, H, nb, Q_BLOCK, d).transpose(2, 0, 1, 3, 4)

    def block(args):
        qi, i = args
        ki = lax.dynamic_slice_in_dim(kp, i * Q_BLOCK, span, axis=2)
        vi = lax.dynamic_slice_in_dim(vp, i * Q_BLOCK, span, axis=2)
        kpos = i * Q_BLOCK - SB_WINDOW + jnp.arange(span)
        allowed = band & (kpos >= 0)[None, :]
        z = jnp.einsum('bhqd,bhkd->bhqk', qi, ki).astype(jnp.float32) * scale
        log_keep = jnp.where(allowed, jax.nn.log_sigmoid(-z), 0.0)
        lk = log_keep.reshape(Bsz, H, Q_BLOCK, nkb, Q_BLOCK)
        within = jnp.einsum('bhqnj,js->bhqns', lk, tri)
        tot = jnp.sum(lk, axis=-1)
        later_blocks = lax.cumsum(tot, axis=3, reverse=True) - tot
        after = (within + later_blocks[..., None]).reshape(Bsz, H, Q_BLOCK, span)
        w = jnp.where(allowed, jnp.exp(jax.nn.log_sigmoid(z) + after), 0.0)
        return jnp.einsum('bhqk,bhkd->bhqd', w.astype(vi.dtype), vi)

    out = lax.map(block, (qb, jnp.arange(nb)))
    return out.transpose(1, 2, 0, 3, 4).reshape(Bsz, H, S, d)


def differential_attention(q, k, v, rel_bias, lam, subln_g, lam_init):
    Bsz, H, _, S, d = q.shape
    nb = S // Q_BLOCK
    span = DIFF_WINDOW + Q_BLOCK
    scale = d ** -0.5
    kp = jnp.pad(k, ((0, 0), (0, 0), (0, 0), (DIFF_WINDOW, 0), (0, 0)))
    vp = jnp.pad(v, ((0, 0), (0, 0), (DIFF_WINDOW, 0), (0, 0)))
    a_key = jnp.arange(span) - DIFF_WINDOW
    a_qry = jnp.arange(Q_BLOCK)
    rel = a_key[None, :] - a_qry[:, None]
    bias = rel_bias.astype(jnp.float32)[t5_bucket(rel)]
    bias = bias.transpose(2, 0, 1)[None, :, None]
    kc = (a_key // CHUNK)[None, :]
    qc = (a_qry // CHUNK)[:, None]
    band = (kc <= qc) & (kc >= qc - DIFF_WINDOW // CHUNK)
    qb = q.reshape(Bsz, H, 2, nb, Q_BLOCK, d).transpose(3, 0, 1, 2, 4, 5)

    def block(args):
        qi, i = args
        ki = lax.dynamic_slice_in_dim(kp, i * Q_BLOCK, span, axis=3)
        vi = lax.dynamic_slice_in_dim(vp, i * Q_BLOCK, span, axis=2)
        kpos = i * Q_BLOCK - DIFF_WINDOW + jnp.arange(span)
        allowed = band & (kpos >= 0)[None, :]
        logits = jnp.einsum('bhmqd,bhmkd->bhmqk', qi, ki).astype(jnp.float32) * scale
        p = jax.nn.softmax(jnp.where(allowed, logits + bias, NEG_INF), axis=-1)
        attn = p[:, :, 0] - lam * p[:, :, 1]
        return jnp.einsum('bhqk,bhke->bhqe', attn.astype(vi.dtype), vi)

    out = lax.map(block, (qb, jnp.arange(nb)))
    out = out.transpose(1, 2, 0, 3, 4).reshape(Bsz, H, S, 2 * d)
    return rms_norm(out, subln_g) * (1.0 - lam_init)


def rg_lru_branch(xb, gate, conv_w, conv_b, w_gate, b_gate, lam):
    Bsz, S, W = xb.shape
    xc = (causal_dwconv(xb, conv_w) + conv_b).astype(jnp.float32)
    xblk = xc.reshape(Bsz, S, LRU_BLOCKS, LRU_BW)
    g = jnp.einsum('bsnc,gncd->gbsnd', xblk, w_gate.astype(jnp.float32)).reshape(2, Bsz, S, W)
    g = jax.nn.sigmoid(g + b_gate.astype(jnp.float32)[:, None, None, :])
    r, i = g[0], g[1]
    log_a = -LRU_C * r * jax.nn.softplus(-lam.astype(jnp.float32))
    a = jnp.exp(log_a)
    b = jnp.sqrt(-jnp.expm1(2.0 * log_a)) * (i * xc)

    def combine(left, right):
        a1, b1 = left
        a2, b2 = right
        return a1 * a2, a2 * b1 + b2

    _, h = lax.associative_scan(combine, (a, b), axis=1)
    return (h * jax.nn.gelu(gate.astype(jnp.float32), approximate=True)).astype(xb.dtype)


def conv_geglu_ffn(h, w_up, conv_w, w_down):
    u = causal_dwconv(h @ w_up, conv_w)
    g, up = jnp.split(u, 2, axis=-1)
    return (jax.nn.gelu(g, approximate=True) * up) @ w_down


def setup_inputs(seed: int = 0) -> dict:
    key = jax.random.key(seed)
    ks = jax.random.split(key, 17)
    f32 = jnp.float32
    nrm = lambda k, s: jax.random.normal(k, s, dtype=f32)
    u = jax.random.uniform(ks[11], (DEPTH, GROUP_W), dtype=f32, minval=0.9, maxval=0.999)
    a0 = u ** (1.0 / LRU_C)
    return {
        "x": nrm(ks[0], (BATCH, SEQ, D_MODEL)),
        "norm_gains": 1.0 + 0.01 * nrm(ks[1], (DEPTH, 4, D_MODEL)),
        "w_in": nrm(ks[2], (DEPTH, D_MODEL, IN_COLS)) * D_MODEL ** -0.5,
        "w_out": nrm(ks[3], (DEPTH, D_MIX, D_MODEL)) * D_MIX ** -0.5,
        "rel_bias": 0.5 * nrm(ks[4], (N_BUCKETS, DIFF_HEADS)),
        "diff_lambda": 0.1 * nrm(ks[5], (DEPTH, 4, HEAD_DIM)),
        "diff_subln_g": 1.0 + 0.01 * nrm(ks[6], (DEPTH, 2 * HEAD_DIM)),
        "lru_conv_w": nrm(ks[7], (DEPTH, LRU_CONV, GROUP_W)) * LRU_CONV ** -0.5,
        "lru_conv_b": 0.01 * nrm(ks[8], (DEPTH, GROUP_W)),
        "lru_w_gate": nrm(ks[9], (DEPTH, 2, LRU_BLOCKS, LRU_BW, LRU_BW)) * LRU_BW ** -0.5,
        "lru_b_gate": 0.01 * nrm(ks[10], (DEPTH, 2, GROUP_W)),
        "lru_lambda": jnp.log(a0) - jnp.log1p(-a0),
        "sc_conv_w": nrm(ks[12], (DEPTH, SC_CONV, GROUP_W)) * SC_CONV ** -0.5,
        "ffn_w_up": nrm(ks[13], (DEPTH, D_MODEL, 2 * D_FF)) * D_MODEL ** -0.5,
        "ffn_conv_w": nrm(ks[14], (DEPTH, FFN_CONV, 2 * D_FF)) * FFN_CONV ** -0.5,
        "ffn_w_down": nrm(ks[15], (DEPTH, D_FF, D_MODEL)) * D_FF ** -0.5,
    }


def reference(x, norm_gains, w_in, w_out, rel_bias, diff_lambda, diff_subln_g,
              lru_conv_w, lru_conv_b, lru_w_gate, lru_b_gate, lru_lambda,
              sc_conv_w, ffn_w_up, ffn_conv_w, ffn_w_down):
    Bsz, S, _ = x.shape
    for l in range(DEPTH):
        lam_init = 0.8 - 0.6 * math.exp(-0.3 * l)
        h = rms_norm(x, norm_gains[l, 0])
        (sb_q, sb_k, sb_v, df_q, df_k, df_v,
         lru_x, lru_g, sc_b, sc_c, sc_x) = jnp.split(h @ w_in[l], N_IN_BLOCKS, axis=-1)

        to_heads = lambda t: t.reshape(Bsz, S, SB_HEADS, HEAD_DIM).transpose(0, 2, 1, 3)
        y_sb = stick_breaking_attention(to_heads(sb_q), to_heads(sb_k), to_heads(sb_v))
        y_sb = y_sb.transpose(0, 2, 1, 3).reshape(Bsz, S, GROUP_W)

        to_pairs = lambda t: t.reshape(Bsz, S, DIFF_HEADS, 2, HEAD_DIM).transpose(0, 2, 3, 1, 4)
        lv = diff_lambda[l].astype(jnp.float32)
        lam = jnp.exp(jnp.sum(lv[0] * lv[1])) - jnp.exp(jnp.sum(lv[2] * lv[3])) + lam_init
        dv = df_v.reshape(Bsz, S, DIFF_HEADS, 2 * HEAD_DIM).transpose(0, 2, 1, 3)
        y_df = differential_attention(to_pairs(df_q), to_pairs(df_k), dv, rel_bias, lam,
                                      diff_subln_g[l], lam_init)
        y_df = y_df.transpose(0, 2, 1, 3).reshape(Bsz, S, GROUP_W)

        y_lru = rg_lru_branch(lru_x, lru_g, lru_conv_w[l], lru_conv_b[l],
                              lru_w_gate[l], lru_b_gate[l], lru_lambda[l])

        y_sc = sc_b * causal_dwconv(sc_c * sc_x, sc_conv_w[l])

        mixed = jnp.concatenate([y_sb, y_df.astype(x.dtype), y_lru, y_sc], axis=-1) @ w_out[l]
        x = x + rms_norm(mixed, norm_gains[l, 1])
        h = rms_norm(x, norm_gains[l, 2])
        x = x + rms_norm(conv_geglu_ffn(h, ffn_w_up[l], ffn_conv_w[l], ffn_w_down[l]),
                         norm_gains[l, 3])
    return x
```

```python
import functools
import math

import numpy as np
import jax
import jax.numpy as jnp
from jax import lax
from jax.experimental import pallas as pl
from jax.experimental.pallas import tpu as pltpu

F32 = jnp.float32
BF16 = jnp.bfloat16

LANE = 128
HEAD_DIM = 64
GROUP_W = 512
GROUP_LANES = GROUP_W // LANE
N_IN_BLOCKS = 11
Q_BLOCK = 128
WINDOW = 1024
SPAN = WINDOW + Q_BLOCK
N_KEY_BLOCKS = SPAN // Q_BLOCK
ATT_TILE = 512
WIN_TILES = WINDOW // ATT_TILE + 1
CHUNK = 64
N_BUCKETS = 32
MAX_DISTANCE = 128
LRU_C = 8.0
LRU_CONV = 4
SC_CONV = 3
FFN_CONV = 3
HALO = 16
NORM_EPS = 1e-6
NEG_INF = -1e30

(SB_Q, SB_K, SB_V, DF_Q, DF_K, DF_V, LRU_X, LRU_G, SC_B, SC_C, SC_X) = range(N_IN_BLOCKS)


def _params(semantics, vmem_mib):
    return pltpu.CompilerParams(dimension_semantics=semantics, vmem_limit_bytes=vmem_mib << 20)


def _rms(v, gain):
    return v * lax.rsqrt(jnp.mean(v * v, axis=-1, keepdims=True) + NORM_EPS) * gain


def _nt_dot(a, b):
    return lax.dot_general(a, b, (((1,), (1,)), ((), ())), preferred_element_type=F32)


def _inproj_kernel(x_ref, g_ref, w_ref, o_ref, h_ref):
    @pl.when(pl.program_id(1) == 0)
    def _():
        h_ref[...] = _rms(x_ref[...], g_ref[...]).astype(BF16)

    res = jnp.dot(h_ref[...], w_ref[...], preferred_element_type=F32)
    for c in range(o_ref.shape[0]):
        o_ref[c] = res[:, c * LANE:(c + 1) * LANE].astype(BF16)


def _inproj(x2, gain, w, tm=1024, tn=512):
    n, d = x2.shape
    cols = w.shape[1]
    return pl.pallas_call(
        _inproj_kernel,
        grid=(n // tm, cols // tn),
        in_specs=[
            pl.BlockSpec((tm, d), lambda i, j: (i, 0)),
            pl.BlockSpec((1, d), lambda i, j: (0, 0)),
            pl.BlockSpec((d, tn), lambda i, j: (0, j)),
        ],
        out_specs=pl.BlockSpec((tn // LANE, tm, LANE), lambda i, j: (j, i, 0)),
        out_shape=jax.ShapeDtypeStruct((cols // LANE, n, LANE), BF16),
        scratch_shapes=[pltpu.VMEM((tm, d), BF16)],
        compiler_params=_params(("parallel", "arbitrary"), 40),
        name="inproj",
    )(x2, gain, w)


def _window_specs(group0, n_tiles):
    specs = []
    for back in range(WIN_TILES - 1, -1, -1):
        specs.append(pl.BlockSpec(
            (None, ATT_TILE, LANE),
            lambda b, g, i, back=back: (group0 + g, b * n_tiles + jnp.maximum(i - back, 0), 0)))
    return specs


def _fill_window(dst, tiles):
    for t, ref in enumerate(tiles):
        dst[t * ATT_TILE:(t + 1) * ATT_TILE, :] = ref[...]


def _mask_missing_tiles(set_cols):
    i = pl.program_id(2)
    for missing in range(1, WIN_TILES):
        @pl.when(i == WIN_TILES - 1 - missing)
        def _(missing=missing):
            set_cols(missing * ATT_TILE)


def _half_select(lane, half):
    return (lane < HEAD_DIM) if half == 0 else (lane >= HEAD_DIM)


def _sb_kernel(q_ref, k2_ref, k1_ref, k0_ref, v2_ref, v1_ref, v0_ref, tri_ref, o_ref,
               kwin, vwin, zfull):
    _fill_window(kwin, (k2_ref, k1_ref, k0_ref))
    _fill_window(vwin, (v2_ref, v1_ref, v0_ref))
    lane = lax.broadcasted_iota(jnp.int32, (1, LANE), 1)
    row = lax.broadcasted_iota(jnp.int32, (Q_BLOCK, Q_BLOCK), 0)
    col = lax.broadcasted_iota(jnp.int32, (Q_BLOCK, Q_BLOCK), 1)
    newest_mask = jnp.where(col < row, 0.0, NEG_INF)
    oldest_mask = jnp.where(col >= row, 0.0, NEG_INF)
    q2 = q_ref[...] * (HEAD_DIM ** -0.5)
    tri = tri_ref[...]
    n_q = ATT_TILE // Q_BLOCK
    head_out = []
    for half in range(2):
        qm = jnp.where(_half_select(lane, half), q2, jnp.zeros_like(q2))
        zfull[...] = _nt_dot(qm, kwin[...])

        def _set(ncols):
            zfull[:, 0:ncols] = jnp.full((ATT_TILE, ncols), NEG_INF, F32)
        _mask_missing_tiles(_set)

        outs = []
        for r in range(n_q):
            rows = slice(r * Q_BLOCK, (r + 1) * Q_BLOCK)
            later = jnp.zeros((Q_BLOCK, Q_BLOCK), F32)
            w_parts = [None] * N_KEY_BLOCKS
            for kb in range(N_KEY_BLOCKS - 1, -1, -1):
                z = zfull[rows, (r + kb) * Q_BLOCK:(r + kb + 1) * Q_BLOCK]
                if kb == N_KEY_BLOCKS - 1:
                    z = z + newest_mask
                elif kb == 0:
                    z = z + oldest_mask
                log_keep = jnp.minimum(-z, 0.0) - jnp.log1p(jnp.exp(-jnp.abs(z)))
                sums = jnp.dot(log_keep.astype(BF16), tri, preferred_element_type=F32)
                after = sums[:, :Q_BLOCK] + later
                w_parts[kb] = jnp.exp(z + log_keep + after).astype(BF16)
                later = later + sums[:, Q_BLOCK:]
            w = jnp.concatenate(w_parts, axis=1)
            outs.append(jnp.dot(w, vwin[r * Q_BLOCK:r * Q_BLOCK + SPAN, :],
                                preferred_element_type=F32))
        head_out.append(outs)
    for r in range(n_q):
        o_ref[r * Q_BLOCK:(r + 1) * Q_BLOCK, :] = jnp.where(
            lane < HEAD_DIM, head_out[0][r], head_out[1][r]).astype(BF16)


def _sb_attention(p, tri, batch, seq):
    n = batch * seq
    n_tiles = seq // ATT_TILE
    tile = lambda g0: pl.BlockSpec((None, ATT_TILE, LANE),
                                   lambda b, g, i: (g0 + g, b * n_tiles + i, 0))
    return pl.pallas_call(
        _sb_kernel,
        grid=(batch, GROUP_LANES, n_tiles),
        in_specs=[tile(SB_Q * GROUP_LANES)]
        + _window_specs(SB_K * GROUP_LANES, n_tiles)
        + _window_specs(SB_V * GROUP_LANES, n_tiles)
        + [pl.BlockSpec((Q_BLOCK, 2 * Q_BLOCK), lambda b, g, i: (0, 0))],
        out_specs=tile(0),
        out_shape=jax.ShapeDtypeStruct((GROUP_LANES, n, LANE), BF16),
        scratch_shapes=[pltpu.VMEM((WIN_TILES * ATT_TILE, LANE), BF16),
                        pltpu.VMEM((WIN_TILES * ATT_TILE, LANE), BF16),
                        pltpu.VMEM((ATT_TILE, WIN_TILES * ATT_TILE), F32)],
        compiler_params=_params(("parallel", "parallel", "arbitrary"), 32),
        name="sb_attention",
    )(*([p] * 7), tri)


def _bias_kernel(rb_ref, bucket_ref, band_ref, o_ref):
    bucket = bucket_ref[...]
    band = band_ref[...] > 0
    for h in range(o_ref.shape[0]):
        acc = jnp.zeros(bucket.shape, F32)
        for b in range(N_BUCKETS):
            acc = jnp.where(bucket == b, rb_ref[b, h], acc)
        o_ref[h] = jnp.where(band, acc, NEG_INF)


def _t5_bucket(rel):
    nb = N_BUCKETS // 2
    max_exact = nb // 2
    ret = jnp.where(rel > 0, nb, 0)
    n = jnp.abs(rel)
    large = max_exact + (jnp.log(jnp.maximum(n, 1).astype(jnp.float32) / max_exact)
                         / math.log(MAX_DISTANCE / max_exact) * (nb - max_exact)).astype(jnp.int32)
    large = jnp.minimum(large, nb - 1)
    return ret + jnp.where(n < max_exact, n, large)


def _diff_bias_table(rel_bias):
    heads = rel_bias.shape[1]
    a_key = jnp.arange(SPAN) - WINDOW
    a_qry = jnp.arange(Q_BLOCK)
    bucket = _t5_bucket(a_key[None, :] - a_qry[:, None]).astype(jnp.int32)
    kc = (a_key // CHUNK)[None, :]
    qc = (a_qry // CHUNK)[:, None]
    band = ((kc <= qc) & (kc >= qc - WINDOW // CHUNK)).astype(jnp.int32)
    return pl.pallas_call(
        _bias_kernel,
        in_specs=[pl.BlockSpec(memory_space=pltpu.SMEM),
                  pl.BlockSpec((Q_BLOCK, SPAN), lambda: (0, 0)),
                  pl.BlockSpec((Q_BLOCK, SPAN), lambda: (0, 0))],
        out_specs=pl.BlockSpec((heads, Q_BLOCK, SPAN), lambda: (0, 0, 0)),
        out_shape=jax.ShapeDtypeStruct((heads, Q_BLOCK, SPAN), F32),
        name="diff_bias_table",
    )(rel_bias.astype(F32), bucket, band)


def _diff_kernel(q_ref, k2_ref, k1_ref, k0_ref, v2_ref, v1_ref, v0_ref, bias_ref, lam_ref, g_ref,
                 o_ref, kwin, vwin, lfull, *, lam_init):
    _fill_window(kwin, (k2_ref, k1_ref, k0_ref))
    _fill_window(vwin, (v2_ref, v1_ref, v0_ref))
    lane = lax.broadcasted_iota(jnp.int32, (1, LANE), 1)
    q2 = q_ref[...] * (HEAD_DIM ** -0.5)
    lv = lam_ref[...]
    lam = (jnp.exp(jnp.sum(lv[0:1] * lv[1:2], axis=-1, keepdims=True))
           - jnp.exp(jnp.sum(lv[2:3] * lv[3:4], axis=-1, keepdims=True)) + lam_init)
    for m in range(2):
        qm = jnp.where(_half_select(lane, m), q2, jnp.zeros_like(q2))
        lfull[m] = _nt_dot(qm, kwin[...])

    def _set(ncols):
        lfull[:, :, 0:ncols] = jnp.full((2, ATT_TILE, ncols), NEG_INF, F32)
    _mask_missing_tiles(_set)

    bias = bias_ref[...]
    gain = g_ref[...] * (1.0 - lam_init)
    for r in range(ATT_TILE // Q_BLOCK):
        rows = slice(r * Q_BLOCK, (r + 1) * Q_BLOCK)
        probs = []
        for m in range(2):
            logits = lfull[m, rows, r * Q_BLOCK:r * Q_BLOCK + SPAN] + bias
            e = jnp.exp(logits - jnp.max(logits, axis=-1, keepdims=True))
            probs.append((e, jnp.sum(e, axis=-1, keepdims=True)))
        attn = probs[0][0] * (1.0 / probs[0][1]) - probs[1][0] * (lam / probs[1][1])
        out = jnp.dot(attn.astype(BF16), vwin[r * Q_BLOCK:r * Q_BLOCK + SPAN, :],
                      preferred_element_type=F32)
        o_ref[rows, :] = _rms(out, gain).astype(BF16)


def _diff_attention(p, bias, lam_params, subln_gain, lam_init, batch, seq):
    n = batch * seq
    n_tiles = seq // ATT_TILE
    heads = bias.shape[0]
    tile = lambda g0: pl.BlockSpec((None, ATT_TILE, LANE),
                                   lambda b, g, i: (g0 + g, b * n_tiles + i, 0))
    return pl.pallas_call(
        functools.partial(_diff_kernel, lam_init=lam_init),
        grid=(batch, heads, n_tiles),
        in_specs=[tile(DF_Q * GROUP_LANES)]
        + _window_specs(DF_K * GROUP_LANES, n_tiles)
        + _window_specs(DF_V * GROUP_LANES, n_tiles)
        + [pl.BlockSpec((None, Q_BLOCK, SPAN), lambda b, g, i: (g, 0, 0)),
           pl.BlockSpec(lam_params.shape, lambda b, g, i: (0, 0)),
           pl.BlockSpec((1, LANE), lambda b, g, i: (0, 0))],
        out_specs=tile(0),
        out_shape=jax.ShapeDtypeStruct((heads, n, LANE), BF16),
        scratch_shapes=[pltpu.VMEM((WIN_TILES * ATT_TILE, LANE), BF16),
                        pltpu.VMEM((WIN_TILES * ATT_TILE, LANE), BF16),
                        pltpu.VMEM((2, ATT_TILE, WIN_TILES * ATT_TILE), F32)],
        compiler_params=_params(("parallel", "parallel", "arbitrary"), 40),
        name="diff_attention",
    )(*([p] * 7), bias, lam_params, subln_gain)


def _softplus(v):
    return jnp.maximum(v, 0.0) + jnp.log1p(jnp.exp(-jnp.abs(v)))


def _expm1(y):
    u = jnp.exp(y)
    is_one = u == 1.0
    near_zero = jnp.where(is_one, y, (u - 1.0) * y / jnp.where(is_one, 1.0, jnp.log(u)))
    return jnp.where(y > -0.5, near_zero, u - 1.0)


def _recurrent_kernel(x_ref, gate_ref, scb_ref, scc_ref, scx_ref,
                      cw_ref, cb_ref, wg_ref, bg_ref, lam_ref, scw_ref,
                      ylru_ref, ysc_ref,
                      xext, sext, a_s, b_s, h_s, carry):
    t = x_ref.shape[1]

    @pl.when(pl.program_id(1) == 0)
    def _():
        xext[0:HALO, :] = jnp.zeros((HALO, GROUP_W), F32)
        sext[0:HALO, :] = jnp.zeros((HALO, GROUP_W), F32)
        carry[...] = jnp.zeros(carry.shape, F32)

    for c in range(GROUP_LANES):
        lanes = slice(c * LANE, (c + 1) * LANE)
        xext[HALO:, lanes] = x_ref[c].astype(F32)
        sext[HALO:, lanes] = scc_ref[c].astype(F32) * scx_ref[c].astype(F32)

    for c in range(GROUP_LANES):
        lanes = slice(c * LANE, (c + 1) * LANE)
        xc = cb_ref[:, lanes]
        for k in range(LRU_CONV):
            off = HALO - (LRU_CONV - 1) + k
            xc = xc + xext[off:off + t, lanes] * cw_ref[k:k + 1, lanes]
        gates = jax.nn.sigmoid(
            jnp.dot(xc.astype(BF16), wg_ref[c], preferred_element_type=F32) + bg_ref[c])
        r_gate = gates[:, :LANE]
        i_gate = gates[:, LANE:]
        log_a = (-LRU_C) * r_gate * _softplus(-lam_ref[:, lanes])
        a_s[:, lanes] = jnp.exp(log_a)
        b_s[:, lanes] = jnp.sqrt(-_expm1(2.0 * log_a)) * (i_gate * xc)

        sc = sext[HALO - (SC_CONV - 1):HALO - (SC_CONV - 1) + t, lanes] * scw_ref[0:1, lanes]
        for k in range(1, SC_CONV):
            off = HALO - (SC_CONV - 1) + k
            sc = sc + sext[off:off + t, lanes] * scw_ref[k:k + 1, lanes]
        ysc_ref[c] = (scb_ref[c].astype(F32) * sc).astype(BF16)

    xext[0:HALO, :] = xext[t:t + HALO, :]
    sext[0:HALO, :] = sext[t:t + HALO, :]

    sub = lax.broadcasted_iota(jnp.int32, (8, GROUP_W), 0)

    def step(j, h_prev):
        rows = pl.ds(pl.multiple_of(j * 8, 8), 8)
        a = a_s[rows, :]
        b = b_s[rows, :]
        for d in (1, 2, 4):
            keep = sub >= d
            b = a * jnp.where(keep, pltpu.roll(b, d, 0), 0.0) + b
            a = a * jnp.where(keep, pltpu.roll(a, d, 0), 1.0)
        h = a * h_prev + b
        h_s[rows, :] = h
        return jnp.broadcast_to(h[7:8, :], (8, GROUP_W))

    carry[...] = lax.fori_loop(0, t // 8, step, carry[...])

    for c in range(GROUP_LANES):
        lanes = slice(c * LANE, (c + 1) * LANE)
        ylru_ref[c] = (h_s[:, lanes]
                       * jax.nn.gelu(gate_ref[c].astype(F32), approximate=True)).astype(BF16)


def _recurrent(p, conv_w, conv_b, w_gate, b_gate, lam, sc_w, batch, seq, t=512):
    n = batch * seq
    n_tiles = seq // t
    blk = lambda which: pl.BlockSpec((GROUP_LANES, t, LANE),
                                     lambda b, i: (which, b * n_tiles + i, 0))
    full = lambda a: pl.BlockSpec(a.shape, lambda b, i: (0,) * a.ndim)
    small = (conv_w, conv_b, w_gate, b_gate, lam, sc_w)
    out = jax.ShapeDtypeStruct((GROUP_LANES, n, LANE), BF16)
    return pl.pallas_call(
        _recurrent_kernel,
        grid=(batch, n_tiles),
        in_specs=[blk(LRU_X), blk(LRU_G), blk(SC_B), blk(SC_C), blk(SC_X)] + [full(a) for a in small],
        out_specs=[blk(0), blk(0)],
        out_shape=[out, out],
        scratch_shapes=[pltpu.VMEM((t + HALO, GROUP_W), F32),
                        pltpu.VMEM((t + HALO, GROUP_W), F32),
                        pltpu.VMEM((t, GROUP_W), F32),
                        pltpu.VMEM((t, GROUP_W), F32),
                        pltpu.VMEM((t, GROUP_W), F32),
                        pltpu.VMEM((8, GROUP_W), F32)],
        compiler_params=_params(("parallel", "arbitrary"), 32),
        name="recurrent",
    )(p, p, p, p, p, *small)


def _gate_weights(w_gate, b_gate):
    per_group = LANE // w_gate.shape[-1]
    zeros = jnp.zeros(w_gate.shape[-2:], w_gate.dtype)
    groups = []
    for c in range(GROUP_LANES):
        halves = []
        for gate in range(2):
            blocks = [w_gate[gate, c * per_group + j] for j in range(per_group)]
            halves.append(jnp.block([[blocks[a] if a == b else zeros for b in range(per_group)]
                                     for a in range(per_group)]))
        groups.append(jnp.concatenate(halves, axis=1))
    wg = jnp.stack(groups).astype(BF16)
    bg = jnp.concatenate([b_gate[0].reshape(GROUP_LANES, 1, LANE),
                          b_gate[1].reshape(GROUP_LANES, 1, LANE)], axis=-1).astype(F32)
    return wg, bg


def _outproj_kernel(sb_ref, df_ref, lru_ref, sc_ref, w_ref, x_ref, g1_ref, g2_ref,
                    xo_ref, h_ref, mixed):
    for a, ref in enumerate((sb_ref, df_ref, lru_ref, sc_ref)):
        for c in range(GROUP_LANES):
            col = (a * GROUP_LANES + c) * LANE
            mixed[:, col:col + LANE] = ref[c]
    y = jnp.dot(mixed[...], w_ref[...], preferred_element_type=F32)
    x_new = x_ref[...] + _rms(y, g1_ref[...])
    xo_ref[...] = x_new
    h_ref[...] = _rms(x_new, g2_ref[...]).astype(BF16)


def _outproj(y_sb, y_df, y_lru, y_sc, w, x2, g1, g2, tm=512):
    n, d = x2.shape
    blk = pl.BlockSpec((GROUP_LANES, tm, LANE), lambda i: (0, i, 0))
    row = pl.BlockSpec((tm, d), lambda i: (i, 0))
    vec = pl.BlockSpec((1, d), lambda i: (0, 0))
    return pl.pallas_call(
        _outproj_kernel,
        grid=(n // tm,),
        in_specs=[blk, blk, blk, blk, pl.BlockSpec(w.shape, lambda i: (0, 0)), row, vec, vec],
        out_specs=[row, row],
        out_shape=[jax.ShapeDtypeStruct((n, d), F32), jax.ShapeDtypeStruct((n, d), BF16)],
        scratch_shapes=[pltpu.VMEM((tm, w.shape[0]), BF16)],
        compiler_params=_params(("parallel",), 48),
        name="outproj",
    )(y_sb, y_df, y_lru, y_sc, w, x2, g1, g2)


def _ffn_kernel(h_ref, halo_ref, wg_ref, wu_ref, cg_ref, cu_ref, wd_ref, x_ref, g_ref, o_ref,
                hext, acc, *, tiles_per_seq):
    i = pl.program_id(0)
    j = pl.program_id(1)
    tm = h_ref.shape[0]

    @pl.when(j == 0)
    def _():
        first = (i % tiles_per_seq) == 0
        halo = halo_ref[...]
        hext[0:HALO, :] = jnp.where(first, jnp.zeros_like(halo), halo)
        hext[HALO:, :] = h_ref[...]
        acc[...] = jnp.zeros(acc.shape, F32)

    def conv(w_ref, cw_ref):
        u = jnp.dot(hext[...], w_ref[...], preferred_element_type=F32)
        out = u[HALO - (FFN_CONV - 1):HALO - (FFN_CONV - 1) + tm] * cw_ref[0:1, :]
        for k in range(1, FFN_CONV):
            off = HALO - (FFN_CONV - 1) + k
            out = out + u[off:off + tm] * cw_ref[k:k + 1, :]
        return out

    act = jax.nn.gelu(conv(wg_ref, cg_ref), approximate=True) * conv(wu_ref, cu_ref)
    acc[...] += jnp.dot(act.astype(BF16), wd_ref[...], preferred_element_type=F32)

    @pl.when(j == pl.num_programs(1) - 1)
    def _():
        o_ref[...] = x_ref[...] + _rms(acc[...], g_ref[...])


def _ffn(h2, w_up, conv_w, w_down, x2, gain, seq, tm=512, tf=512):
    n, d = x2.shape
    d_ff = w_down.shape[0]
    n_f = d_ff // tf
    row = pl.BlockSpec((tm, d), lambda i, j: (i, 0))
    return pl.pallas_call(
        functools.partial(_ffn_kernel, tiles_per_seq=seq // tm),
        grid=(n // tm, n_f),
        in_specs=[
            row,
            pl.BlockSpec((HALO, d), lambda i, j: (jnp.maximum(i * (tm // HALO) - 1, 0), 0)),
            pl.BlockSpec((d, tf), lambda i, j: (0, j)),
            pl.BlockSpec((d, tf), lambda i, j: (0, n_f + j)),
            pl.BlockSpec((FFN_CONV, tf), lambda i, j: (0, j)),
            pl.BlockSpec((FFN_CONV, tf), lambda i, j: (0, n_f + j)),
            pl.BlockSpec((tf, d), lambda i, j: (j, 0)),
            row,
            pl.BlockSpec((1, d), lambda i, j: (0, 0)),
        ],
        out_specs=row,
        out_shape=jax.ShapeDtypeStruct((n, d), F32),
        scratch_shapes=[pltpu.VMEM((tm + HALO, d), BF16), pltpu.VMEM((tm, d), F32)],
        compiler_params=_params(("parallel", "arbitrary"), 48),
        name="ffn",
    )(h2, h2, w_up, w_up, conv_w, conv_w, w_down, x2, gain)


def _tri_and_ones():
    later = np.arange(Q_BLOCK)[:, None] > np.arange(Q_BLOCK)[None, :]
    return jnp.asarray(np.concatenate([later, np.ones_like(later)], axis=1), BF16)


def kernel(x, norm_gains, w_in, w_out, rel_bias, diff_lambda, diff_subln_g, lru_conv_w, lru_conv_b,
           lru_w_gate, lru_b_gate, lru_lambda, sc_conv_w, ffn_w_up, ffn_conv_w, ffn_w_down):
    batch, seq, d = x.shape
    depth = w_in.shape[0]
    assert seq % ATT_TILE == 0 and w_in.shape[2] == N_IN_BLOCKS * GROUP_W
    x2 = x.reshape(batch * seq, d).astype(F32)
    tri = _tri_and_ones()
    bias = _diff_bias_table(rel_bias)
    row = lambda v: v.reshape(1, -1).astype(F32)
    for l in range(depth):
        lam_init = 0.8 - 0.6 * math.exp(-0.3 * l)
        p = _inproj(x2, row(norm_gains[l, 0]), w_in[l].astype(BF16))
        y_sb = _sb_attention(p, tri, batch, seq)
        y_df = _diff_attention(p, bias, diff_lambda[l].astype(F32), row(diff_subln_g[l]),
                               lam_init, batch, seq)
        wg, bg = _gate_weights(lru_w_gate[l], lru_b_gate[l])
        y_lru, y_sc = _recurrent(p, lru_conv_w[l].astype(F32), row(lru_conv_b[l]), wg, bg,
                                 row(lru_lambda[l]), sc_conv_w[l].astype(F32), batch, seq)
        x2, h2 = _outproj(y_sb, y_df, y_lru, y_sc, w_out[l].astype(BF16), x2,
                          row(norm_gains[l, 1]), row(norm_gains[l, 2]))
        x2 = _ffn(h2, ffn_w_up[l].astype(BF16), ffn_conv_w[l].astype(F32),
                  ffn_w_down[l].astype(BF16), x2, row(norm_gains[l, 3]), seq)
    return x2.reshape(batch, seq, d).astype(x.dtype)
```

```python
import functools
import math

import numpy as np
import jax
import jax.numpy as jnp
from jax import lax
from jax.experimental import pallas as pl
from jax.experimental.pallas import tpu as pltpu

F32 = jnp.float32
BF16 = jnp.bfloat16

LANE = 128
MXU_COLS = 256
FFN_ROWS = 512
HEAD_DIM = 64
GROUP_W = 512
GROUP_LANES = GROUP_W // LANE
N_IN_BLOCKS = 11
Q_BLOCK = 128
WINDOW = 1024
SPAN = WINDOW + Q_BLOCK
N_KEY_BLOCKS = SPAN // Q_BLOCK
SB_GROUP = 2
ATT_TILE = 512
WIN_TILES = WINDOW // ATT_TILE + 1
CHUNK = 64
N_BUCKETS = 32
MAX_DISTANCE = 128
LRU_C = 8.0
LRU_CONV = 4
SC_CONV = 3
FFN_CONV = 3
HALO = 16
NORM_EPS = 1e-6
NEG_INF = -1e30

(SB_Q, SB_K, SB_V, DF_Q, DF_K, DF_V, LRU_X, LRU_G, SC_B, SC_C, SC_X) = range(N_IN_BLOCKS)


def _params(semantics, vmem_mib):
    return pltpu.CompilerParams(dimension_semantics=semantics, vmem_limit_bytes=vmem_mib << 20)


def _rms(v, gain):
    return v * lax.rsqrt(jnp.mean(v * v, axis=-1, keepdims=True) + NORM_EPS) * gain


def _nt_dot(a, b):
    return lax.dot_general(a, b, (((1,), (1,)), ((), ())), preferred_element_type=F32)


def _inproj_kernel(x_ref, g_ref, w_ref, o_ref, h_ref):
    @pl.when(pl.program_id(1) == 0)
    def _():
        h_ref[...] = _rms(x_ref[...], g_ref[...]).astype(BF16)

    res = jnp.dot(h_ref[...], w_ref[...], preferred_element_type=F32)
    for c in range(o_ref.shape[0]):
        o_ref[c] = res[:, c * LANE:(c + 1) * LANE].astype(BF16)


def _inproj(x2, gain, w, tm=1024, tn=512):
    n, d = x2.shape
    cols = w.shape[1]
    return pl.pallas_call(
        _inproj_kernel,
        grid=(n // tm, cols // tn),
        in_specs=[
            pl.BlockSpec((tm, d), lambda i, j: (i, 0)),
            pl.BlockSpec((1, d), lambda i, j: (0, 0)),
            pl.BlockSpec((d, tn), lambda i, j: (0, j)),
        ],
        out_specs=pl.BlockSpec((tn // LANE, tm, LANE), lambda i, j: (j, i, 0)),
        out_shape=jax.ShapeDtypeStruct((cols // LANE, n, LANE), BF16),
        scratch_shapes=[pltpu.VMEM((tm, d), BF16)],
        compiler_params=_params(("parallel", "arbitrary"), 40),
        name="inproj",
    )(x2, gain, w)


def _window_specs(group0, n_tiles):
    specs = []
    for back in range(WIN_TILES - 1, -1, -1):
        specs.append(pl.BlockSpec(
            (None, ATT_TILE, LANE),
            lambda b, g, i, back=back: (group0 + g, b * n_tiles + jnp.maximum(i - back, 0), 0)))
    return specs


def _fill_window(dst, tiles):
    for t, ref in enumerate(tiles):
        dst[t * ATT_TILE:(t + 1) * ATT_TILE, :] = ref[...]


def _mask_missing_tiles(set_cols):
    i = pl.program_id(2)
    for missing in range(1, WIN_TILES):
        @pl.when(i == WIN_TILES - 1 - missing)
        def _(missing=missing):
            set_cols(missing * ATT_TILE)


def _half_select(lane, half):
    return (lane < HEAD_DIM) if half == 0 else (lane >= HEAD_DIM)


def _sb_kernel(q_ref, k2_ref, k1_ref, k0_ref, v2_ref, v1_ref, v0_ref, tri_ref, o_ref,
               kwin, vwin, zfull):
    _fill_window(kwin, (k2_ref, k1_ref, k0_ref))
    _fill_window(vwin, (v2_ref, v1_ref, v0_ref))
    lane = lax.broadcasted_iota(jnp.int32, (1, LANE), 1)
    row = lax.broadcasted_iota(jnp.int32, (Q_BLOCK, SB_GROUP * Q_BLOCK), 0)
    col = lax.broadcasted_iota(jnp.int32, (Q_BLOCK, SB_GROUP * Q_BLOCK), 1)
    newest_mask = jnp.where(col < row + (SB_GROUP - 1) * Q_BLOCK, 0.0, NEG_INF)
    oldest_mask = jnp.where(lax.broadcasted_iota(jnp.int32, (Q_BLOCK, Q_BLOCK), 1)
                            >= lax.broadcasted_iota(jnp.int32, (Q_BLOCK, Q_BLOCK), 0),
                            0.0, NEG_INF)
    q2 = q_ref[...] * (HEAD_DIM ** -0.5)
    n_q = ATT_TILE // Q_BLOCK
    groups = [(hi - SB_GROUP, hi) for hi in range(N_KEY_BLOCKS, SB_GROUP - 1, -SB_GROUP)]
    if groups[-1][0] > 0:
        groups.append((0, groups[-1][0]))
    head_out = []
    for half in range(2):
        qm = jnp.where(_half_select(lane, half), q2, jnp.zeros_like(q2))
        zfull[...] = _nt_dot(qm, kwin[...])

        def _set(ncols):
            zfull[:, 0:ncols] = jnp.full((ATT_TILE, ncols), NEG_INF, F32)
        _mask_missing_tiles(_set)

        outs = []
        for r in range(n_q):
            rows = slice(r * Q_BLOCK, (r + 1) * Q_BLOCK)
            later = None
            w_parts = []
            for lo, hi in groups:
                width = (hi - lo) * Q_BLOCK
                z = zfull[rows, (r + lo) * Q_BLOCK:(r + hi) * Q_BLOCK]
                if hi == N_KEY_BLOCKS:
                    z = z + newest_mask[:, :width]
                if lo == 0:
                    oldest = z[:, :Q_BLOCK] + oldest_mask
                    z = oldest if width == Q_BLOCK else jnp.concatenate([oldest, z[:, Q_BLOCK:]], axis=1)
                log_keep = jnp.minimum(-z, 0.0) - jnp.log(1.0 + jnp.exp(-jnp.abs(z)))
                within = jnp.dot(log_keep.astype(BF16), tri_ref[0:width, 0:width],
                                 preferred_element_type=F32)
                after = within if later is None else within + jnp.concatenate(
                    [later] * (hi - lo), axis=1)
                w_parts.append(jnp.exp(z + log_keep + after).astype(BF16))
                total = jnp.broadcast_to(jnp.sum(log_keep, axis=-1, keepdims=True), (Q_BLOCK, LANE))
                later = total if later is None else later + total
            w = jnp.concatenate(w_parts[::-1], axis=1)
            outs.append(jnp.dot(w, vwin[r * Q_BLOCK:r * Q_BLOCK + SPAN, :],
                                preferred_element_type=F32))
        head_out.append(outs)
    for r in range(n_q):
        o_ref[r * Q_BLOCK:(r + 1) * Q_BLOCK, :] = jnp.where(
            lane < HEAD_DIM, head_out[0][r], head_out[1][r]).astype(BF16)


def _sb_attention(p, tri, batch, seq):
    n = batch * seq
    n_tiles = seq // ATT_TILE
    tile = lambda g0: pl.BlockSpec((None, ATT_TILE, LANE),
                                   lambda b, g, i: (g0 + g, b * n_tiles + i, 0))
    return pl.pallas_call(
        _sb_kernel,
        grid=(batch, GROUP_LANES, n_tiles),
        in_specs=[tile(SB_Q * GROUP_LANES)]
        + _window_specs(SB_K * GROUP_LANES, n_tiles)
        + _window_specs(SB_V * GROUP_LANES, n_tiles)
        + [pl.BlockSpec(tri.shape, lambda b, g, i: (0, 0))],
        out_specs=tile(0),
        out_shape=jax.ShapeDtypeStruct((GROUP_LANES, n, LANE), BF16),
        scratch_shapes=[pltpu.VMEM((WIN_TILES * ATT_TILE, LANE), BF16),
                        pltpu.VMEM((WIN_TILES * ATT_TILE, LANE), BF16),
                        pltpu.VMEM((ATT_TILE, WIN_TILES * ATT_TILE), F32)],
        compiler_params=_params(("parallel", "parallel", "arbitrary"), 32),
        name="sb_attention",
    )(*([p] * 7), tri)


def _bias_kernel(rb_ref, bucket_ref, band_ref, o_ref):
    bucket = bucket_ref[...]
    band = band_ref[...] > 0
    for h in range(o_ref.shape[0]):
        acc = jnp.zeros(bucket.shape, F32)
        for b in range(N_BUCKETS):
            acc = jnp.where(bucket == b, rb_ref[b, h], acc)
        o_ref[h] = jnp.where(band, acc, NEG_INF)


def _t5_bucket(rel):
    nb = N_BUCKETS // 2
    max_exact = nb // 2
    ret = jnp.where(rel > 0, nb, 0)
    n = jnp.abs(rel)
    large = max_exact + (jnp.log(jnp.maximum(n, 1).astype(jnp.float32) / max_exact)
                         / math.log(MAX_DISTANCE / max_exact) * (nb - max_exact)).astype(jnp.int32)
    large = jnp.minimum(large, nb - 1)
    return ret + jnp.where(n < max_exact, n, large)


def _diff_bias_table(rel_bias):
    heads = rel_bias.shape[1]
    a_key = jnp.arange(SPAN) - WINDOW
    a_qry = jnp.arange(Q_BLOCK)
    bucket = _t5_bucket(a_key[None, :] - a_qry[:, None]).astype(jnp.int32)
    kc = (a_key // CHUNK)[None, :]
    qc = (a_qry // CHUNK)[:, None]
    band = ((kc <= qc) & (kc >= qc - WINDOW // CHUNK)).astype(jnp.int32)
    return pl.pallas_call(
        _bias_kernel,
        in_specs=[pl.BlockSpec(memory_space=pltpu.SMEM),
                  pl.BlockSpec((Q_BLOCK, SPAN), lambda: (0, 0)),
                  pl.BlockSpec((Q_BLOCK, SPAN), lambda: (0, 0))],
        out_specs=pl.BlockSpec((heads, Q_BLOCK, SPAN), lambda: (0, 0, 0)),
        out_shape=jax.ShapeDtypeStruct((heads, Q_BLOCK, SPAN), F32),
        name="diff_bias_table",
    )(rel_bias.astype(F32), bucket, band)


def _diff_kernel(q_ref, k2_ref, k1_ref, k0_ref, v2_ref, v1_ref, v0_ref, bias_ref, lam_ref, g_ref,
                 o_ref, kwin, vwin, lfull, *, lam_init):
    _fill_window(kwin, (k2_ref, k1_ref, k0_ref))
    _fill_window(vwin, (v2_ref, v1_ref, v0_ref))
    lane = lax.broadcasted_iota(jnp.int32, (1, LANE), 1)
    q2 = q_ref[...] * (HEAD_DIM ** -0.5)
    lv = lam_ref[...]
    lam = (jnp.exp(jnp.sum(lv[0:1] * lv[1:2], axis=-1, keepdims=True))
           - jnp.exp(jnp.sum(lv[2:3] * lv[3:4], axis=-1, keepdims=True)) + lam_init)
    for m in range(2):
        qm = jnp.where(_half_select(lane, m), q2, jnp.zeros_like(q2))
        lfull[m] = _nt_dot(qm, kwin[...])

    def _set(ncols):
        lfull[:, :, 0:ncols] = jnp.full((2, ATT_TILE, ncols), NEG_INF, F32)
    _mask_missing_tiles(_set)

    bias = bias_ref[...]
    gain = g_ref[...] * (1.0 - lam_init)
    for r in range(ATT_TILE // Q_BLOCK):
        rows = slice(r * Q_BLOCK, (r + 1) * Q_BLOCK)
        probs = []
        for m in range(2):
            logits = lfull[m, rows, r * Q_BLOCK:r * Q_BLOCK + SPAN] + bias
            e = jnp.exp(logits - jnp.max(logits, axis=-1, keepdims=True))
            probs.append((e, jnp.sum(e, axis=-1, keepdims=True)))
        attn = probs[0][0] * (1.0 / probs[0][1]) - probs[1][0] * (lam / probs[1][1])
        out = jnp.dot(attn.astype(BF16), vwin[r * Q_BLOCK:r * Q_BLOCK + SPAN, :],
                      preferred_element_type=F32)
        o_ref[rows, :] = _rms(out, gain).astype(BF16)


def _diff_attention(p, bias, lam_params, subln_gain, lam_init, batch, seq):
    n = batch * seq
    n_tiles = seq // ATT_TILE
    heads = bias.shape[0]
    tile = lambda g0: pl.BlockSpec((None, ATT_TILE, LANE),
                                   lambda b, g, i: (g0 + g, b * n_tiles + i, 0))
    return pl.pallas_call(
        functools.partial(_diff_kernel, lam_init=lam_init),
        grid=(batch, heads, n_tiles),
        in_specs=[tile(DF_Q * GROUP_LANES)]
        + _window_specs(DF_K * GROUP_LANES, n_tiles)
        + _window_specs(DF_V * GROUP_LANES, n_tiles)
        + [pl.BlockSpec((None, Q_BLOCK, SPAN), lambda b, g, i: (g, 0, 0)),
           pl.BlockSpec(lam_params.shape, lambda b, g, i: (0, 0)),
           pl.BlockSpec((1, LANE), lambda b, g, i: (0, 0))],
        out_specs=tile(0),
        out_shape=jax.ShapeDtypeStruct((heads, n, LANE), BF16),
        scratch_shapes=[pltpu.VMEM((WIN_TILES * ATT_TILE, LANE), BF16),
                        pltpu.VMEM((WIN_TILES * ATT_TILE, LANE), BF16),
                        pltpu.VMEM((2, ATT_TILE, WIN_TILES * ATT_TILE), F32)],
        compiler_params=_params(("parallel", "parallel", "arbitrary"), 40),
        name="diff_attention",
    )(*([p] * 7), bias, lam_params, subln_gain)


def _softplus(v):
    return jnp.maximum(v, 0.0) + jnp.log1p(jnp.exp(-jnp.abs(v)))


def _expm1(y):
    u = jnp.exp(y)
    is_one = u == 1.0
    near_zero = jnp.where(is_one, y, (u - 1.0) * y / jnp.where(is_one, 1.0, jnp.log(u)))
    return jnp.where(y > -0.5, near_zero, u - 1.0)


def _causal_conv(padded, taps):
    n_taps = taps.shape[0]
    out = padded[8:] * taps[n_taps - 1:n_taps]
    for k in range(n_taps - 1):
        out = out + pltpu.roll(padded, n_taps - 1 - k, 0)[8:] * taps[k:k + 1]
    return out


def _recurrent_kernel(x_ref, gate_ref, scb_ref, scc_ref, scx_ref,
                      cw_ref, cb_ref, wg_ref, bg_ref, lam_ref, scw_ref,
                      ylru_ref, ysc_ref,
                      xext, sext, a_s, b_s, h_s, carry):
    t = x_ref.shape[1]

    @pl.when(pl.program_id(1) == 0)
    def _():
        xext[0:HALO, :] = jnp.zeros((HALO, GROUP_W), F32)
        sext[0:HALO, :] = jnp.zeros((HALO, GROUP_W), F32)
        carry[...] = jnp.zeros(carry.shape, F32)

    for c in range(GROUP_LANES):
        lanes = slice(c * LANE, (c + 1) * LANE)
        xext[HALO:, lanes] = x_ref[c].astype(F32)
        sext[HALO:, lanes] = scc_ref[c].astype(F32) * scx_ref[c].astype(F32)

    for c in range(GROUP_LANES):
        lanes = slice(c * LANE, (c + 1) * LANE)
        xc = cb_ref[:, lanes] + _causal_conv(xext[HALO - 8:, lanes], cw_ref[:, lanes])
        gates = jax.nn.sigmoid(
            jnp.dot(xc.astype(BF16), wg_ref[c], preferred_element_type=F32) + bg_ref[c])
        r_gate = gates[:, :LANE]
        i_gate = gates[:, LANE:]
        log_a = (-LRU_C) * r_gate * _softplus(-lam_ref[:, lanes])
        a_s[:, lanes] = jnp.exp(log_a)
        b_s[:, lanes] = jnp.sqrt(-_expm1(2.0 * log_a)) * (i_gate * xc)

        sc = _causal_conv(sext[HALO - 8:, lanes], scw_ref[:, lanes])
        ysc_ref[c] = (scb_ref[c].astype(F32) * sc).astype(BF16)

    xext[0:HALO, :] = xext[t:t + HALO, :]
    sext[0:HALO, :] = sext[t:t + HALO, :]

    sub = lax.broadcasted_iota(jnp.int32, (8, GROUP_W), 0)

    def step(j, h_prev):
        rows = pl.ds(pl.multiple_of(j * 8, 8), 8)
        a = a_s[rows, :]
        b = b_s[rows, :]
        for d in (1, 2, 4):
            keep = sub >= d
            b = a * jnp.where(keep, pltpu.roll(b, d, 0), 0.0) + b
            a = a * jnp.where(keep, pltpu.roll(a, d, 0), 1.0)
        h = a * h_prev + b
        h_s[rows, :] = h
        return jnp.broadcast_to(h[7:8, :], (8, GROUP_W))

    carry[...] = lax.fori_loop(0, t // 8, step, carry[...])

    for c in range(GROUP_LANES):
        lanes = slice(c * LANE, (c + 1) * LANE)
        ylru_ref[c] = (h_s[:, lanes]
                       * jax.nn.gelu(gate_ref[c].astype(F32), approximate=True)).astype(BF16)


def _recurrent(p, conv_w, conv_b, w_gate, b_gate, lam, sc_w, batch, seq, t=512):
    n = batch * seq
    n_tiles = seq // t
    blk = lambda which: pl.BlockSpec((GROUP_LANES, t, LANE),
                                     lambda b, i: (which, b * n_tiles + i, 0))
    full = lambda a: pl.BlockSpec(a.shape, lambda b, i: (0,) * a.ndim)
    small = (conv_w, conv_b, w_gate, b_gate, lam, sc_w)
    out = jax.ShapeDtypeStruct((GROUP_LANES, n, LANE), BF16)
    return pl.pallas_call(
        _recurrent_kernel,
        grid=(batch, n_tiles),
        in_specs=[blk(LRU_X), blk(LRU_G), blk(SC_B), blk(SC_C), blk(SC_X)] + [full(a) for a in small],
        out_specs=[blk(0), blk(0)],
        out_shape=[out, out],
        scratch_shapes=[pltpu.VMEM((t + HALO, GROUP_W), F32),
                        pltpu.VMEM((t + HALO, GROUP_W), F32),
                        pltpu.VMEM((t, GROUP_W), F32),
                        pltpu.VMEM((t, GROUP_W), F32),
                        pltpu.VMEM((t, GROUP_W), F32),
                        pltpu.VMEM((8, GROUP_W), F32)],
        compiler_params=_params(("parallel", "arbitrary"), 32),
        name="recurrent",
    )(p, p, p, p, p, *small)


def _gate_weights(w_gate, b_gate):
    per_group = LANE // w_gate.shape[-1]
    zeros = jnp.zeros(w_gate.shape[-2:], w_gate.dtype)
    groups = []
    for c in range(GROUP_LANES):
        halves = []
        for gate in range(2):
            blocks = [w_gate[gate, c * per_group + j] for j in range(per_group)]
            halves.append(jnp.block([[blocks[a] if a == b else zeros for b in range(per_group)]
                                     for a in range(per_group)]))
        groups.append(jnp.concatenate(halves, axis=1))
    wg = jnp.stack(groups).astype(BF16)
    bg = jnp.concatenate([b_gate[0].reshape(GROUP_LANES, 1, LANE),
                          b_gate[1].reshape(GROUP_LANES, 1, LANE)], axis=-1).astype(F32)
    return wg, bg


def _outproj_kernel(sb_ref, df_ref, lru_ref, sc_ref, w_ref, x_ref, g1_ref, g2_ref,
                    xo_ref, h_ref, mixed):
    for a, ref in enumerate((sb_ref, df_ref, lru_ref, sc_ref)):
        for c in range(GROUP_LANES):
            col = (a * GROUP_LANES + c) * LANE
            mixed[:, col:col + LANE] = ref[c]
    y = jnp.dot(mixed[...], w_ref[...], preferred_element_type=F32)
    x_new = x_ref[...] + _rms(y, g1_ref[...])
    xo_ref[...] = x_new
    h_ref[...] = _rms(x_new, g2_ref[...]).astype(BF16)


def _outproj(y_sb, y_df, y_lru, y_sc, w, x2, g1, g2, tm=512):
    n, d = x2.shape
    blk = pl.BlockSpec((GROUP_LANES, tm, LANE), lambda i: (0, i, 0))
    row = pl.BlockSpec((tm, d), lambda i: (i, 0))
    vec = pl.BlockSpec((1, d), lambda i: (0, 0))
    return pl.pallas_call(
        _outproj_kernel,
        grid=(n // tm,),
        in_specs=[blk, blk, blk, blk, pl.BlockSpec(w.shape, lambda i: (0, 0)), row, vec, vec],
        out_specs=[row, row],
        out_shape=[jax.ShapeDtypeStruct((n, d), F32), jax.ShapeDtypeStruct((n, d), BF16)],
        scratch_shapes=[pltpu.VMEM((tm, w.shape[0]), BF16)],
        compiler_params=_params(("parallel",), 48),
        name="outproj",
    )(y_sb, y_df, y_lru, y_sc, w, x2, g1, g2)


def _ffn_up_kernel(h_ref, wg_ref, wu_ref, cg_ref, cu_ref, o_ref, ug_s, uu_s, carry_g, carry_u,
                   *, tiles_per_seq):
    i = pl.program_id(0)
    j = pl.program_id(1)
    tm, tf = o_ref.shape
    @pl.when(i == 0)
    def _():
        carry_g[j] = jnp.zeros(carry_g.shape[1:], F32)
        carry_u[j] = jnp.zeros(carry_u.shape[1:], F32)

    first = (i % tiles_per_seq) == 0
    ug_s[0:8, :] = jnp.where(first, 0.0, carry_g[j])
    uu_s[0:8, :] = jnp.where(first, 0.0, carry_u[j])

    n_blocks = tm // FFN_ROWS

    def project(c, k):
        cols = slice(c * MXU_COLS, (c + 1) * MXU_COLS)
        rb = k % n_blocks
        w_ref, dst_ref = ((wg_ref, ug_s), (wu_ref, uu_s))[k // n_blocks]
        dst_ref[8 + rb * FFN_ROWS:8 + (rb + 1) * FFN_ROWS, cols] = jnp.dot(
            h_ref[rb * FFN_ROWS:(rb + 1) * FFN_ROWS, :], w_ref[:, cols], preferred_element_type=F32)

    strip = tm // (2 * n_blocks)

    def activate(c, k):
        cols = slice(c * MXU_COLS, (c + 1) * MXU_COLS)
        src = slice(k * strip, (k + 1) * strip + 8)
        gate = _causal_conv(ug_s[src, cols], cg_ref[:, cols])
        up = _causal_conv(uu_s[src, cols], cu_ref[:, cols])
        o_ref[k * strip:(k + 1) * strip, cols] = (
            jax.nn.gelu(gate, approximate=True) * up).astype(BF16)

    n_chunks = tf // MXU_COLS
    for c in range(n_chunks + 1):
        for k in range(2 * n_blocks):
            if c < n_chunks:
                project(c, k)
            if c > 0:
                activate(c - 1, k)
    carry_g[j] = ug_s[tm:tm + 8, :]
    carry_u[j] = uu_s[tm:tm + 8, :]


def _ffn_up(h2, w_up, conv_w, seq, tm=1024, tf=1024):
    n, d = h2.shape
    d_ff = w_up.shape[1] // 2
    n_f = d_ff // tf
    return pl.pallas_call(
        functools.partial(_ffn_up_kernel, tiles_per_seq=seq // tm),
        grid=(n // tm, n_f),
        in_specs=[
            pl.BlockSpec((tm, d), lambda i, j: (i, 0)),
            pl.BlockSpec((d, tf), lambda i, j: (0, j)),
            pl.BlockSpec((d, tf), lambda i, j: (0, n_f + j)),
            pl.BlockSpec((FFN_CONV, tf), lambda i, j: (0, j)),
            pl.BlockSpec((FFN_CONV, tf), lambda i, j: (0, n_f + j)),
        ],
        out_specs=pl.BlockSpec((tm, tf), lambda i, j: (i, j)),
        out_shape=jax.ShapeDtypeStruct((n, d_ff), BF16),
        scratch_shapes=[pltpu.VMEM((8 + tm, tf), F32), pltpu.VMEM((8 + tm, tf), F32),
                        pltpu.VMEM((n_f, 8, tf), F32), pltpu.VMEM((n_f, 8, tf), F32)],
        compiler_params=_params(("arbitrary", "arbitrary"), 48),
        name="ffn_up",
    )(h2, w_up, w_up, conv_w, conv_w)


def _ffn_down_kernel(a_ref, w_ref, x_ref, g_ref, o_ref):
    y = jnp.dot(a_ref[...], w_ref[...], preferred_element_type=F32)
    o_ref[...] = x_ref[...] + _rms(y, g_ref[...])


def _ffn_down(act, w_down, x2, gain, tm=512):
    n, d = x2.shape
    row = pl.BlockSpec((tm, d), lambda i: (i, 0))
    return pl.pallas_call(
        _ffn_down_kernel,
        grid=(n // tm,),
        in_specs=[
            pl.BlockSpec((tm, act.shape[1]), lambda i: (i, 0)),
            pl.BlockSpec(w_down.shape, lambda i: (0, 0), pipeline_mode=pl.Buffered(1)),
            row,
            pl.BlockSpec((1, d), lambda i: (0, 0)),
        ],
        out_specs=row,
        out_shape=jax.ShapeDtypeStruct((n, d), F32),
        compiler_params=_params(("parallel",), 48),
        name="ffn_down",
    )(act, w_down, x2, gain)


def _later_matrix():
    idx = np.arange(SB_GROUP * Q_BLOCK)
    return jnp.asarray(idx[:, None] > idx[None, :], BF16)


def kernel(x, norm_gains, w_in, w_out, rel_bias, diff_lambda, diff_subln_g, lru_conv_w, lru_conv_b,
           lru_w_gate, lru_b_gate, lru_lambda, sc_conv_w, ffn_w_up, ffn_conv_w, ffn_w_down):
    batch, seq, d = x.shape
    depth = w_in.shape[0]
    assert seq % ATT_TILE == 0 and w_in.shape[2] == N_IN_BLOCKS * GROUP_W
    x2 = x.reshape(batch * seq, d).astype(F32)
    tri = _later_matrix()
    bias = _diff_bias_table(rel_bias)
    row = lambda v: v.reshape(1, -1).astype(F32)
    for l in range(depth):
        lam_init = 0.8 - 0.6 * math.exp(-0.3 * l)
        p = _inproj(x2, row(norm_gains[l, 0]), w_in[l].astype(BF16))
        y_sb = _sb_attention(p, tri, batch, seq)
        y_df = _diff_attention(p, bias, diff_lambda[l].astype(F32), row(diff_subln_g[l]),
                               lam_init, batch, seq)
        wg, bg = _gate_weights(lru_w_gate[l], lru_b_gate[l])
        y_lru, y_sc = _recurrent(p, lru_conv_w[l].astype(F32), row(lru_conv_b[l]), wg, bg,
                                 row(lru_lambda[l]), sc_conv_w[l].astype(F32), batch, seq)
        x2, h2 = _outproj(y_sb, y_df, y_lru, y_sc, w_out[l].astype(BF16), x2,
                          row(norm_gains[l, 1]), row(norm_gains[l, 2]))
        act = _ffn_up(h2, ffn_w_up[l].astype(BF16), ffn_conv_w[l].astype(F32), seq)
        x2 = _ffn_down(act, ffn_w_down[l].astype(BF16), x2, row(norm_gains[l, 3]))
    return x2.reshape(batch, seq, d).astype(x.dtype)
```

```python
import functools
import math

import numpy as np
import jax
import jax.numpy as jnp
from jax import lax
from jax.experimental import pallas as pl
from jax.experimental.pallas import tpu as pltpu

F32 = jnp.float32
BF16 = jnp.bfloat16

LANE = 128
MXU_COLS = 256
FFN_ROWS = 512
HEAD_DIM = 64
GROUP_W = 512
GROUP_LANES = GROUP_W // LANE
N_IN_BLOCKS = 11
Q_BLOCK = 128
WINDOW = 1024
SPAN = WINDOW + Q_BLOCK
N_KEY_BLOCKS = SPAN // Q_BLOCK
SB_GROUP = 2
SB_NEAR_BLOCKS = 3
EXIT_LOG_WEIGHT = -110.0
ATT_TILE = 512
WIN_TILES = WINDOW // ATT_TILE + 1
CHUNK = 64
N_BUCKETS = 32
MAX_DISTANCE = 128
LRU_C = 8.0
LRU_CONV = 4
SC_CONV = 3
FFN_CONV = 3
HALO = 16
NORM_EPS = 1e-6
NEG_INF = -1e30

(SB_Q, SB_K, SB_V, DF_Q, DF_K, DF_V, LRU_X, LRU_G, SC_B, SC_C, SC_X) = range(N_IN_BLOCKS)


def _params(semantics, vmem_mib):
    return pltpu.CompilerParams(dimension_semantics=semantics, vmem_limit_bytes=vmem_mib << 20)


def _rms(v, gain):
    return v * lax.rsqrt(jnp.mean(v * v, axis=-1, keepdims=True) + NORM_EPS) * gain


def _nt_dot(a, b):
    return lax.dot_general(a, b, (((1,), (1,)), ((), ())), preferred_element_type=F32)


def _inproj_kernel(x_ref, g_ref, w_ref, o_ref, h_ref):
    @pl.when(pl.program_id(1) == 0)
    def _():
        h_ref[...] = _rms(x_ref[...], g_ref[...]).astype(BF16)

    res = jnp.dot(h_ref[...], w_ref[...], preferred_element_type=F32)
    for c in range(o_ref.shape[0]):
        o_ref[c] = res[:, c * LANE:(c + 1) * LANE].astype(BF16)


def _inproj(x2, gain, w, tm=1024, tn=512):
    n, d = x2.shape
    cols = w.shape[1]
    return pl.pallas_call(
        _inproj_kernel,
        grid=(n // tm, cols // tn),
        in_specs=[
            pl.BlockSpec((tm, d), lambda i, j: (i, 0)),
            pl.BlockSpec((1, d), lambda i, j: (0, 0)),
            pl.BlockSpec((d, tn), lambda i, j: (0, j)),
        ],
        out_specs=pl.BlockSpec((tn // LANE, tm, LANE), lambda i, j: (j, i, 0)),
        out_shape=jax.ShapeDtypeStruct((cols // LANE, n, LANE), BF16),
        scratch_shapes=[pltpu.VMEM((tm, d), BF16)],
        compiler_params=_params(("parallel", "arbitrary"), 40),
        name="inproj",
    )(x2, gain, w)


def _window_specs(group0, n_tiles):
    specs = []
    for back in range(WIN_TILES - 1, -1, -1):
        specs.append(pl.BlockSpec(
            (None, ATT_TILE, LANE),
            lambda b, g, i, back=back: (group0 + g, b * n_tiles + jnp.maximum(i - back, 0), 0)))
    return specs


def _next_tile_spec(group0, n_tiles):
    return pl.BlockSpec(
        (None, ATT_TILE, LANE),
        lambda b, g, i: (group0 + g, b * n_tiles + jnp.minimum(i + 1, n_tiles - 1), 0))


def _fill_window(dst, tiles):
    for t, ref in enumerate(tiles):
        dst[t * ATT_TILE:(t + 1) * ATT_TILE, :] = ref[...]


def _mask_missing_tiles(set_cols):
    i = pl.program_id(2)
    for missing in range(1, WIN_TILES):
        @pl.when(i == WIN_TILES - 1 - missing)
        def _(missing=missing):
            set_cols(missing * ATT_TILE)


def _half_select(lane, half):
    return (lane < HEAD_DIM) if half == 0 else (lane >= HEAD_DIM)


def _key_groups(lo, hi):
    groups = [(max(top - SB_GROUP, lo), top) for top in range(hi, lo, -SB_GROUP)]
    return groups


def _sb_weights(z_of, groups, later, tri_ref, newest_mask, oldest_mask):
    w_parts = []
    for lo, hi in groups:
        width = (hi - lo) * Q_BLOCK
        z = z_of(lo, hi)
        if hi == N_KEY_BLOCKS:
            z = z + newest_mask[:, newest_mask.shape[1] - width:]
        if lo == 0:
            oldest = z[:, :Q_BLOCK] + oldest_mask
            z = oldest if width == Q_BLOCK else jnp.concatenate([oldest, z[:, Q_BLOCK:]], axis=1)
        log_keep = jnp.minimum(-z, 0.0) - jnp.log(1.0 + jnp.exp(-jnp.abs(z)))
        within = jnp.dot(log_keep.astype(BF16), tri_ref[0:width, 0:width],
                         preferred_element_type=F32)
        after = within if later is None else within + jnp.concatenate([later] * (hi - lo), axis=1)
        w_parts.append(jnp.exp(z + log_keep + after).astype(BF16))
        total = jnp.broadcast_to(jnp.sum(log_keep, axis=-1, keepdims=True), (Q_BLOCK, LANE))
        later = total if later is None else later + total
    return jnp.concatenate(w_parts[::-1], axis=1), later


def _sb_near(znear, vwin, tri_ref, newest_mask, acc_s, later_s, keep_s, arg_s, within_s, newer_s, w_s):
    n_q = ATT_TILE // Q_BLOCK
    first_near = N_KEY_BLOCKS - SB_NEAR_BLOCKS
    near_groups = _key_groups(first_near, N_KEY_BLOCKS)
    cols_of = lambda lo, hi: slice((lo - first_near) * Q_BLOCK, (hi - first_near) * Q_BLOCK)
    pairs = [(half, r) for half in range(2) for r in range(n_q)]
    pair_rows = lambda c: slice(c * Q_BLOCK, (c + 1) * Q_BLOCK)

    worst = [None] * n_q
    for c, (half, r) in enumerate(pairs):
        later = None
        for g, (lo, hi) in enumerate(near_groups):
            z = znear[half, r * Q_BLOCK:(r + 1) * Q_BLOCK,
                      (r + lo - first_near) * Q_BLOCK:(r + hi - first_near) * Q_BLOCK]
            if hi == N_KEY_BLOCKS:
                z = z + newest_mask[:, newest_mask.shape[1] - (hi - lo) * Q_BLOCK:]
            log_keep = jnp.minimum(-z, 0.0) - jnp.log(1.0 + jnp.exp(-jnp.abs(z)))
            keep_s[pair_rows(c), cols_of(lo, hi)] = log_keep.astype(BF16)
            arg_s[pair_rows(c), cols_of(lo, hi)] = z + log_keep
            if g > 0:
                newer_s[g - 1, c] = later
            total = jnp.broadcast_to(jnp.sum(log_keep, axis=-1, keepdims=True), (Q_BLOCK, LANE))
            later = total if later is None else later + total
        later_s[c] = later
        worst[r] = later if worst[r] is None else jnp.maximum(worst[r], later)

    for lo, hi in near_groups:
        width = (hi - lo) * Q_BLOCK
        within_s[:, cols_of(lo, hi)] = jnp.dot(keep_s[:, cols_of(lo, hi)], tri_ref[0:width, 0:width],
                                               preferred_element_type=F32)

    for c, (half, r) in enumerate(pairs):
        for g, (lo, hi) in enumerate(near_groups):
            arg = arg_s[pair_rows(c), cols_of(lo, hi)] + within_s[pair_rows(c), cols_of(lo, hi)]
            if g > 0:
                arg = arg + jnp.concatenate([newer_s[g - 1, c]] * (hi - lo), axis=1)
            w_s[r, half * Q_BLOCK:(half + 1) * Q_BLOCK, cols_of(lo, hi)] = jnp.exp(arg).astype(BF16)

    for r in range(n_q):
        out = jnp.dot(w_s[r], vwin[(r + first_near) * Q_BLOCK:(r + N_KEY_BLOCKS) * Q_BLOCK, :],
                      preferred_element_type=F32)
        for half in range(2):
            acc_s[half, r * Q_BLOCK:(r + 1) * Q_BLOCK, :] = out[half * Q_BLOCK:(half + 1) * Q_BLOCK]

    return worst


def _sb_kernel(q_ref, qn_ref, k2_ref, k1_ref, k0_ref, kn_ref, v2_ref, v1_ref, v0_ref, tri_ref, o_ref,
               kwin, knear, vwin, znear_a, znear_b, zfar, acc_s, later_s, *near_scratch):
    i = pl.program_id(2)
    lane = lax.broadcasted_iota(jnp.int32, (1, LANE), 1)
    row = lax.broadcasted_iota(jnp.int32, (Q_BLOCK, SB_GROUP * Q_BLOCK), 0)
    col = lax.broadcasted_iota(jnp.int32, (Q_BLOCK, SB_GROUP * Q_BLOCK), 1)
    newest_mask = jnp.where(col < row + (SB_GROUP - 1) * Q_BLOCK, 0.0, NEG_INF)
    oldest_mask = jnp.where(lax.broadcasted_iota(jnp.int32, (Q_BLOCK, Q_BLOCK), 1)
                            >= lax.broadcasted_iota(jnp.int32, (Q_BLOCK, Q_BLOCK), 0),
                            0.0, NEG_INF)
    n_q = ATT_TILE // Q_BLOCK
    first_near = N_KEY_BLOCKS - SB_NEAR_BLOCKS
    near_row0 = first_near * Q_BLOCK
    older = (WIN_TILES - 1) * ATT_TILE - near_row0

    def masked_queries(ref):
        q2 = ref[...] * (HEAD_DIM ** -0.5)
        return [jnp.where(_half_select(lane, half), q2, jnp.zeros_like(q2)) for half in range(2)]

    @pl.when(i == 0)
    def _():
        for half, qm in enumerate(masked_queries(q_ref)):
            znear_a[half, :, 0:older] = jnp.full((ATT_TILE, older), NEG_INF, F32)
            znear_a[half, :, older:] = _nt_dot(qm, k0_ref[...])

    def step(cur, nxt):
        knear[0:older, :] = k0_ref[ATT_TILE - older:, :]
        knear[older:, :] = kn_ref[...]
        for half, qm in enumerate(masked_queries(qn_ref)):
            nxt[half] = _nt_dot(qm, knear[...])

        _fill_window(vwin, (v2_ref, v1_ref, v0_ref))
        worst_of = _sb_near(cur, vwin, tri_ref, newest_mask, acc_s, later_s, *near_scratch)

        missing_rows = jnp.maximum(WIN_TILES - 1 - i, 0) * ATT_TILE
        worst = None
        for r in range(n_q):
            has_far_keys = (r + first_near) * Q_BLOCK > missing_rows
            worst_r = jnp.where(has_far_keys, worst_of[r], NEG_INF)
            worst = worst_r if worst is None else jnp.maximum(worst, worst_r)

        @pl.when(jnp.max(worst) >= EXIT_LOG_WEIGHT)
        def _():
            _fill_window(kwin, (k2_ref, k1_ref, k0_ref))
            for half, qm in enumerate(masked_queries(q_ref)):
                zfar[...] = _nt_dot(qm, kwin[0:(n_q - 1 + first_near) * Q_BLOCK, :])

                def _set(ncols):
                    zfar[:, 0:ncols] = jnp.full((ATT_TILE, ncols), NEG_INF, F32)
                _mask_missing_tiles(_set)

                for r in range(n_q):
                    rows = slice(r * Q_BLOCK, (r + 1) * Q_BLOCK)
                    z_of = lambda lo, hi, r=r, rows=rows: zfar[
                        rows, (r + lo) * Q_BLOCK:(r + hi) * Q_BLOCK]
                    w, _ = _sb_weights(z_of, _key_groups(0, first_near), later_s[half * n_q + r],
                                       tri_ref, newest_mask, oldest_mask)
                    acc_s[half, rows, :] += jnp.dot(
                        w, vwin[r * Q_BLOCK:(r + first_near) * Q_BLOCK, :],
                        preferred_element_type=F32)

        o_ref[...] = jnp.where(lane < HEAD_DIM, acc_s[0], acc_s[1]).astype(BF16)

    @pl.when(i % 2 == 0)
    def _():
        step(znear_a, znear_b)

    @pl.when(i % 2 == 1)
    def _():
        step(znear_b, znear_a)


def _sb_attention(p, tri, batch, seq):
    n = batch * seq
    n_tiles = seq // ATT_TILE
    n_q = ATT_TILE // Q_BLOCK
    near_row0 = (N_KEY_BLOCKS - SB_NEAR_BLOCKS) * Q_BLOCK
    far_rows = (n_q - 1 + N_KEY_BLOCKS - SB_NEAR_BLOCKS) * Q_BLOCK
    near_cols = SB_NEAR_BLOCKS * Q_BLOCK
    n_near_groups = len(_key_groups(N_KEY_BLOCKS - SB_NEAR_BLOCKS, N_KEY_BLOCKS))
    tile = lambda g0: pl.BlockSpec((None, ATT_TILE, LANE),
                                   lambda b, g, i: (g0 + g, b * n_tiles + i, 0))
    near_rows = WIN_TILES * ATT_TILE - near_row0
    near_logits = pltpu.VMEM((2, ATT_TILE, near_rows), F32)
    return pl.pallas_call(
        _sb_kernel,
        grid=(batch, GROUP_LANES, n_tiles),
        in_specs=[tile(SB_Q * GROUP_LANES), _next_tile_spec(SB_Q * GROUP_LANES, n_tiles)]
        + _window_specs(SB_K * GROUP_LANES, n_tiles) + [_next_tile_spec(SB_K * GROUP_LANES, n_tiles)]
        + _window_specs(SB_V * GROUP_LANES, n_tiles)
        + [pl.BlockSpec(tri.shape, lambda b, g, i: (0, 0))],
        out_specs=tile(0),
        out_shape=jax.ShapeDtypeStruct((GROUP_LANES, n, LANE), BF16),
        scratch_shapes=[pltpu.VMEM((WIN_TILES * ATT_TILE, LANE), BF16),
                        pltpu.VMEM((near_rows, LANE), BF16),
                        pltpu.VMEM((WIN_TILES * ATT_TILE, LANE), BF16),
                        near_logits, near_logits,
                        pltpu.VMEM((ATT_TILE, far_rows), F32),
                        pltpu.VMEM((2, ATT_TILE, LANE), F32),
                        pltpu.VMEM((2 * n_q, Q_BLOCK, LANE), F32),
                        pltpu.VMEM((2 * ATT_TILE, near_cols), BF16),
                        pltpu.VMEM((2 * ATT_TILE, near_cols), F32),
                        pltpu.VMEM((2 * ATT_TILE, near_cols), F32),
                        pltpu.VMEM((n_near_groups - 1, 2 * n_q, Q_BLOCK, LANE), F32),
                        pltpu.VMEM((n_q, 2 * Q_BLOCK, near_cols), BF16)],
        compiler_params=_params(("parallel", "parallel", "arbitrary"), 32),
        name="sb_attention",
    )(*([p] * 9), tri)


def _bias_kernel(rb_ref, bucket_ref, band_ref, o_ref):
    bucket = bucket_ref[...]
    band = band_ref[...] > 0
    for h in range(o_ref.shape[0]):
        acc = jnp.zeros(bucket.shape, F32)
        for b in range(N_BUCKETS):
            acc = jnp.where(bucket == b, rb_ref[b, h], acc)
        o_ref[h] = jnp.where(band, acc, NEG_INF)


def _t5_bucket(rel):
    nb = N_BUCKETS // 2
    max_exact = nb // 2
    ret = jnp.where(rel > 0, nb, 0)
    n = jnp.abs(rel)
    large = max_exact + (jnp.log(jnp.maximum(n, 1).astype(jnp.float32) / max_exact)
                         / math.log(MAX_DISTANCE / max_exact) * (nb - max_exact)).astype(jnp.int32)
    large = jnp.minimum(large, nb - 1)
    return ret + jnp.where(n < max_exact, n, large)


def _diff_bias_table(rel_bias):
    heads = rel_bias.shape[1]
    a_key = jnp.arange(SPAN) - WINDOW
    a_qry = jnp.arange(Q_BLOCK)
    bucket = _t5_bucket(a_key[None, :] - a_qry[:, None]).astype(jnp.int32)
    kc = (a_key // CHUNK)[None, :]
    qc = (a_qry // CHUNK)[:, None]
    band = ((kc <= qc) & (kc >= qc - WINDOW // CHUNK)).astype(jnp.int32)
    return pl.pallas_call(
        _bias_kernel,
        in_specs=[pl.BlockSpec(memory_space=pltpu.SMEM),
                  pl.BlockSpec((Q_BLOCK, SPAN), lambda: (0, 0)),
                  pl.BlockSpec((Q_BLOCK, SPAN), lambda: (0, 0))],
        out_specs=pl.BlockSpec((heads, Q_BLOCK, SPAN), lambda: (0, 0, 0)),
        out_shape=jax.ShapeDtypeStruct((heads, Q_BLOCK, SPAN), F32),
        name="diff_bias_table",
    )(rel_bias.astype(F32), bucket, band)


def _diff_kernel(q_ref, qn_ref, k2_ref, k1_ref, k0_ref, kn_ref, v2_ref, v1_ref, v0_ref,
                 bias_ref, lam_ref, g_ref, o_ref, kwin, vwin, logits_a, logits_b, *, lam_init):
    i = pl.program_id(2)
    lane = lax.broadcasted_iota(jnp.int32, (1, LANE), 1)
    lv = lam_ref[...]
    lam = (jnp.exp(jnp.sum(lv[0:1] * lv[1:2], axis=-1, keepdims=True))
           - jnp.exp(jnp.sum(lv[2:3] * lv[3:4], axis=-1, keepdims=True)) + lam_init)
    gain = g_ref[...] * (1.0 - lam_init)

    def masked_queries(ref):
        q2 = ref[...] * (HEAD_DIM ** -0.5)
        return [jnp.where(_half_select(lane, m), q2, jnp.zeros_like(q2)) for m in range(2)]

    @pl.when(i == 0)
    def _():
        _fill_window(kwin, (k2_ref, k1_ref, k0_ref))
        ncols = (WIN_TILES - 1) * ATT_TILE
        for m, qm in enumerate(masked_queries(q_ref)):
            logits_a[m, :, 0:ncols] = jnp.full((ATT_TILE, ncols), NEG_INF, F32)
            logits_a[m, :, ncols:] = _nt_dot(qm, kwin[ncols:, :])

    def step(cur, nxt):
        _fill_window(kwin, (k1_ref, k0_ref, kn_ref))
        for m, qm in enumerate(masked_queries(qn_ref)):
            res = _nt_dot(qm, kwin[...])
            nxt[m, :, 0:ATT_TILE] = jnp.where(i == 0, NEG_INF, res[:, 0:ATT_TILE])
            nxt[m, :, ATT_TILE:] = res[:, ATT_TILE:]

        _fill_window(vwin, (v2_ref, v1_ref, v0_ref))
        bias = bias_ref[...]
        for r in range(ATT_TILE // Q_BLOCK):
            rows = slice(r * Q_BLOCK, (r + 1) * Q_BLOCK)
            probs = []
            for m in range(2):
                logits = cur[m, rows, r * Q_BLOCK:r * Q_BLOCK + SPAN] + bias
                e = jnp.exp(logits - jnp.max(logits, axis=-1, keepdims=True))
                probs.append((e, jnp.sum(e, axis=-1, keepdims=True)))
            attn = probs[0][0] * (1.0 / probs[0][1]) - probs[1][0] * (lam / probs[1][1])
            out = jnp.dot(attn.astype(BF16), vwin[r * Q_BLOCK:r * Q_BLOCK + SPAN, :],
                          preferred_element_type=F32)
            o_ref[rows, :] = _rms(out, gain).astype(BF16)

    @pl.when(i % 2 == 0)
    def _():
        step(logits_a, logits_b)

    @pl.when(i % 2 == 1)
    def _():
        step(logits_b, logits_a)


def _diff_attention(p, bias, lam_params, subln_gain, lam_init, batch, seq):
    n = batch * seq
    n_tiles = seq // ATT_TILE
    heads = bias.shape[0]
    tile = lambda g0: pl.BlockSpec((None, ATT_TILE, LANE),
                                   lambda b, g, i: (g0 + g, b * n_tiles + i, 0))
    next_tile = lambda g0: _next_tile_spec(g0, n_tiles)
    logits = pltpu.VMEM((2, ATT_TILE, WIN_TILES * ATT_TILE), F32)
    return pl.pallas_call(
        functools.partial(_diff_kernel, lam_init=lam_init),
        grid=(batch, heads, n_tiles),
        in_specs=[tile(DF_Q * GROUP_LANES), next_tile(DF_Q * GROUP_LANES)]
        + _window_specs(DF_K * GROUP_LANES, n_tiles) + [next_tile(DF_K * GROUP_LANES)]
        + _window_specs(DF_V * GROUP_LANES, n_tiles)
        + [pl.BlockSpec((None, Q_BLOCK, SPAN), lambda b, g, i: (g, 0, 0)),
           pl.BlockSpec(lam_params.shape, lambda b, g, i: (0, 0)),
           pl.BlockSpec((1, LANE), lambda b, g, i: (0, 0))],
        out_specs=tile(0),
        out_shape=jax.ShapeDtypeStruct((heads, n, LANE), BF16),
        scratch_shapes=[pltpu.VMEM((WIN_TILES * ATT_TILE, LANE), BF16),
                        pltpu.VMEM((WIN_TILES * ATT_TILE, LANE), BF16),
                        logits, logits],
        compiler_params=_params(("parallel", "parallel", "arbitrary"), 40),
        name="diff_attention",
    )(*([p] * 9), bias, lam_params, subln_gain)


def _softplus(v):
    return jnp.maximum(v, 0.0) + jnp.log1p(jnp.exp(-jnp.abs(v)))


def _expm1(y):
    u = jnp.exp(y)
    is_one = u == 1.0
    near_zero = jnp.where(is_one, y, (u - 1.0) * y / jnp.where(is_one, 1.0, jnp.log(u)))
    return jnp.where(y > -0.5, near_zero, u - 1.0)


def _causal_conv(padded, taps):
    n_taps = taps.shape[0]
    out = padded[8:] * taps[n_taps - 1:n_taps]
    for k in range(n_taps - 1):
        out = out + pltpu.roll(padded, n_taps - 1 - k, 0)[8:] * taps[k:k + 1]
    return out


def _recurrent_kernel(x_ref, gate_ref, scb_ref, scc_ref, scx_ref,
                      cw_ref, cb_ref, wg_ref, bg_ref, lam_ref, scw_ref,
                      ylru_ref, ysc_ref,
                      xext, sext, a_s, b_s, h_s, carry):
    t = x_ref.shape[1]

    @pl.when(pl.program_id(1) == 0)
    def _():
        xext[0:HALO, :] = jnp.zeros((HALO, GROUP_W), F32)
        sext[0:HALO, :] = jnp.zeros((HALO, GROUP_W), F32)
        carry[...] = jnp.zeros(carry.shape, F32)

    for c in range(GROUP_LANES):
        lanes = slice(c * LANE, (c + 1) * LANE)
        xext[HALO:, lanes] = x_ref[c].astype(F32)
        sext[HALO:, lanes] = scc_ref[c].astype(F32) * scx_ref[c].astype(F32)

    for c in range(GROUP_LANES):
        lanes = slice(c * LANE, (c + 1) * LANE)
        xc = cb_ref[:, lanes] + _causal_conv(xext[HALO - 8:, lanes], cw_ref[:, lanes])
        gates = jax.nn.sigmoid(
            jnp.dot(xc.astype(BF16), wg_ref[c], preferred_element_type=F32) + bg_ref[c])
        r_gate = gates[:, :LANE]
        i_gate = gates[:, LANE:]
        log_a = (-LRU_C) * r_gate * _softplus(-lam_ref[:, lanes])
        a_s[:, lanes] = jnp.exp(log_a)
        b_s[:, lanes] = jnp.sqrt(-_expm1(2.0 * log_a)) * (i_gate * xc)

        sc = _causal_conv(sext[HALO - 8:, lanes], scw_ref[:, lanes])
        ysc_ref[c] = (scb_ref[c].astype(F32) * sc).astype(BF16)

    xext[0:HALO, :] = xext[t:t + HALO, :]
    sext[0:HALO, :] = sext[t:t + HALO, :]

    sub = lax.broadcasted_iota(jnp.int32, (8, GROUP_W), 0)

    def step(j, h_prev):
        rows = pl.ds(pl.multiple_of(j * 8, 8), 8)
        a = a_s[rows, :]
        b = b_s[rows, :]
        for d in (1, 2, 4):
            keep = sub >= d
            b = a * jnp.where(keep, pltpu.roll(b, d, 0), 0.0) + b
            a = a * jnp.where(keep, pltpu.roll(a, d, 0), 1.0)
        h = a * h_prev + b
        h_s[rows, :] = h
        return jnp.broadcast_to(h[7:8, :], (8, GROUP_W))

    carry[...] = lax.fori_loop(0, t // 8, step, carry[...])

    for c in range(GROUP_LANES):
        lanes = slice(c * LANE, (c + 1) * LANE)
        ylru_ref[c] = (h_s[:, lanes]
                       * jax.nn.gelu(gate_ref[c].astype(F32), approximate=True)).astype(BF16)


def _recurrent(p, conv_w, conv_b, w_gate, b_gate, lam, sc_w, batch, seq, t=512):
    n = batch * seq
    n_tiles = seq // t
    blk = lambda which: pl.BlockSpec((GROUP_LANES, t, LANE),
                                     lambda b, i: (which, b * n_tiles + i, 0))
    full = lambda a: pl.BlockSpec(a.shape, lambda b, i: (0,) * a.ndim)
    small = (conv_w, conv_b, w_gate, b_gate, lam, sc_w)
    out = jax.ShapeDtypeStruct((GROUP_LANES, n, LANE), BF16)
    return pl.pallas_call(
        _recurrent_kernel,
        grid=(batch, n_tiles),
        in_specs=[blk(LRU_X), blk(LRU_G), blk(SC_B), blk(SC_C), blk(SC_X)] + [full(a) for a in small],
        out_specs=[blk(0), blk(0)],
        out_shape=[out, out],
        scratch_shapes=[pltpu.VMEM((t + HALO, GROUP_W), F32),
                        pltpu.VMEM((t + HALO, GROUP_W), F32),
                        pltpu.VMEM((t, GROUP_W), F32),
                        pltpu.VMEM((t, GROUP_W), F32),
                        pltpu.VMEM((t, GROUP_W), F32),
                        pltpu.VMEM((8, GROUP_W), F32)],
        compiler_params=_params(("parallel", "arbitrary"), 32),
        name="recurrent",
    )(p, p, p, p, p, *small)


def _gate_weights(w_gate, b_gate):
    per_group = LANE // w_gate.shape[-1]
    zeros = jnp.zeros(w_gate.shape[-2:], w_gate.dtype)
    groups = []
    for c in range(GROUP_LANES):
        halves = []
        for gate in range(2):
            blocks = [w_gate[gate, c * per_group + j] for j in range(per_group)]
            halves.append(jnp.block([[blocks[a] if a == b else zeros for b in range(per_group)]
                                     for a in range(per_group)]))
        groups.append(jnp.concatenate(halves, axis=1))
    wg = jnp.stack(groups).astype(BF16)
    bg = jnp.concatenate([b_gate[0].reshape(GROUP_LANES, 1, LANE),
                          b_gate[1].reshape(GROUP_LANES, 1, LANE)], axis=-1).astype(F32)
    return wg, bg


def _outproj_kernel(sb_ref, df_ref, lru_ref, sc_ref, w_ref, x_ref, g1_ref, g2_ref,
                    xo_ref, h_ref, mixed):
    for a, ref in enumerate((sb_ref, df_ref, lru_ref, sc_ref)):
        for c in range(GROUP_LANES):
            col = (a * GROUP_LANES + c) * LANE
            mixed[:, col:col + LANE] = ref[c]
    y = jnp.dot(mixed[...], w_ref[...], preferred_element_type=F32)
    x_new = x_ref[...] + _rms(y, g1_ref[...])
    xo_ref[...] = x_new
    h_ref[...] = _rms(x_new, g2_ref[...]).astype(BF16)


def _outproj(y_sb, y_df, y_lru, y_sc, w, x2, g1, g2, tm=512):
    n, d = x2.shape
    blk = pl.BlockSpec((GROUP_LANES, tm, LANE), lambda i: (0, i, 0))
    row = pl.BlockSpec((tm, d), lambda i: (i, 0))
    vec = pl.BlockSpec((1, d), lambda i: (0, 0))
    return pl.pallas_call(
        _outproj_kernel,
        grid=(n // tm,),
        in_specs=[blk, blk, blk, blk, pl.BlockSpec(w.shape, lambda i: (0, 0)), row, vec, vec],
        out_specs=[row, row],
        out_shape=[jax.ShapeDtypeStruct((n, d), F32), jax.ShapeDtypeStruct((n, d), BF16)],
        scratch_shapes=[pltpu.VMEM((tm, w.shape[0]), BF16)],
        compiler_params=_params(("parallel",), 48),
        name="outproj",
    )(y_sb, y_df, y_lru, y_sc, w, x2, g1, g2)


def _ffn_up_kernel(h_ref, wg_ref, wu_ref, cg_ref, cu_ref, o_ref, ug_s, uu_s, carry_g, carry_u,
                   *, tiles_per_seq):
    i = pl.program_id(0)
    j = pl.program_id(1)
    tm, tf = o_ref.shape
    @pl.when(i == 0)
    def _():
        carry_g[j] = jnp.zeros(carry_g.shape[1:], F32)
        carry_u[j] = jnp.zeros(carry_u.shape[1:], F32)

    first = (i % tiles_per_seq) == 0
    ug_s[0:8, :] = jnp.where(first, 0.0, carry_g[j])
    uu_s[0:8, :] = jnp.where(first, 0.0, carry_u[j])

    n_blocks = tm // FFN_ROWS

    def project(c, k):
        cols = slice(c * MXU_COLS, (c + 1) * MXU_COLS)
        rb = k % n_blocks
        w_ref, dst_ref = ((wg_ref, ug_s), (wu_ref, uu_s))[k // n_blocks]
        dst_ref[8 + rb * FFN_ROWS:8 + (rb + 1) * FFN_ROWS, cols] = jnp.dot(
            h_ref[rb * FFN_ROWS:(rb + 1) * FFN_ROWS, :], w_ref[:, cols], preferred_element_type=F32)

    strip = tm // (2 * n_blocks)

    def activate(c, k):
        cols = slice(c * MXU_COLS, (c + 1) * MXU_COLS)
        src = slice(k * strip, (k + 1) * strip + 8)
        gate = _causal_conv(ug_s[src, cols], cg_ref[:, cols])
        up = _causal_conv(uu_s[src, cols], cu_ref[:, cols])
        o_ref[k * strip:(k + 1) * strip, cols] = (
            jax.nn.gelu(gate, approximate=True) * up).astype(BF16)

    n_chunks = tf // MXU_COLS
    for c in range(n_chunks + 1):
        for k in range(2 * n_blocks):
            if c < n_chunks:
                project(c, k)
            if c > 0:
                activate(c - 1, k)
    carry_g[j] = ug_s[tm:tm + 8, :]
    carry_u[j] = uu_s[tm:tm + 8, :]


def _ffn_up(h2, w_up, conv_w, seq, tm=1024, tf=1024):
    n, d = h2.shape
    d_ff = w_up.shape[1] // 2
    n_f = d_ff // tf
    assert seq % tm == 0 and d_ff % tf == 0
    return pl.pallas_call(
        functools.partial(_ffn_up_kernel, tiles_per_seq=seq // tm),
        grid=(n // tm, n_f),
        in_specs=[
            pl.BlockSpec((tm, d), lambda i, j: (i, 0)),
            pl.BlockSpec((d, tf), lambda i, j: (0, j)),
            pl.BlockSpec((d, tf), lambda i, j: (0, n_f + j)),
            pl.BlockSpec((FFN_CONV, tf), lambda i, j: (0, j)),
            pl.BlockSpec((FFN_CONV, tf), lambda i, j: (0, n_f + j)),
        ],
        out_specs=pl.BlockSpec((tm, tf), lambda i, j: (i, j)),
        out_shape=jax.ShapeDtypeStruct((n, d_ff), BF16),
        scratch_shapes=[pltpu.VMEM((8 + tm, tf), F32), pltpu.VMEM((8 + tm, tf), F32),
                        pltpu.VMEM((n_f, 8, tf), F32), pltpu.VMEM((n_f, 8, tf), F32)],
        compiler_params=_params(("arbitrary", "arbitrary"), 48),
        name="ffn_up",
    )(h2, w_up, w_up, conv_w, conv_w)


def _ffn_down_kernel(a_ref, w_ref, x_ref, g_ref, o_ref):
    y = jnp.dot(a_ref[...], w_ref[...], preferred_element_type=F32)
    o_ref[...] = x_ref[...] + _rms(y, g_ref[...])


def _ffn_down(act, w_down, x2, gain, tm=512):
    n, d = x2.shape
    row = pl.BlockSpec((tm, d), lambda i: (i, 0))
    return pl.pallas_call(
        _ffn_down_kernel,
        grid=(n // tm,),
        in_specs=[
            pl.BlockSpec((tm, act.shape[1]), lambda i: (i, 0)),
            pl.BlockSpec(w_down.shape, lambda i: (0, 0), pipeline_mode=pl.Buffered(1)),
            row,
            pl.BlockSpec((1, d), lambda i: (0, 0)),
        ],
        out_specs=row,
        out_shape=jax.ShapeDtypeStruct((n, d), F32),
        compiler_params=_params(("parallel",), 48),
        name="ffn_down",
    )(act, w_down, x2, gain)


def _later_matrix():
    idx = np.arange(SB_GROUP * Q_BLOCK)
    return jnp.asarray(idx[:, None] > idx[None, :], BF16)


def kernel(x, norm_gains, w_in, w_out, rel_bias, diff_lambda, diff_subln_g, lru_conv_w, lru_conv_b,
           lru_w_gate, lru_b_gate, lru_lambda, sc_conv_w, ffn_w_up, ffn_conv_w, ffn_w_down):
    batch, seq, d = x.shape
    depth = w_in.shape[0]
    assert seq % ATT_TILE == 0 and w_in.shape[2] == N_IN_BLOCKS * GROUP_W
    x2 = x.reshape(batch * seq, d).astype(F32)
    tri = _later_matrix()
    bias = _diff_bias_table(rel_bias)
    row = lambda v: v.reshape(1, -1).astype(F32)
    for l in range(depth):
        lam_init = 0.8 - 0.6 * math.exp(-0.3 * l)
        p = _inproj(x2, row(norm_gains[l, 0]), w_in[l].astype(BF16))
        y_sb = _sb_attention(p, tri, batch, seq)
        y_df = _diff_attention(p, bias, diff_lambda[l].astype(F32), row(diff_subln_g[l]),
                               lam_init, batch, seq)
        wg, bg = _gate_weights(lru_w_gate[l], lru_b_gate[l])
        y_lru, y_sc = _recurrent(p, lru_conv_w[l].astype(F32), row(lru_conv_b[l]), wg, bg,
                                 row(lru_lambda[l]), sc_conv_w[l].astype(F32), batch, seq)
        x2, h2 = _outproj(y_sb, y_df, y_lru, y_sc, w_out[l].astype(BF16), x2,
                          row(norm_gains[l, 1]), row(norm_gains[l, 2]))
        act = _ffn_up(h2, ffn_w_up[l].astype(BF16), ffn_conv_w[l].astype(F32), seq)
        x2 = _ffn_down(act, ffn_w_down[l].astype(BF16), x2, row(norm_gains[l, 3]))
    return x2.reshape(batch, seq, d).astype(x.dtype)
```

```python
import functools
import math

import numpy as np
import jax
import jax.numpy as jnp
from jax import lax
from jax.experimental import pallas as pl
from jax.experimental.pallas import tpu as pltpu

F32 = jnp.float32
BF16 = jnp.bfloat16

LANE = 128
MXU_COLS = 256
FFN_STRIP = 256
FFN_ROWS = 512
HEAD_DIM = 64
GROUP_W = 512
GROUP_LANES = GROUP_W // LANE
N_IN_BLOCKS = 11
Q_BLOCK = 128
WINDOW = 1024
SPAN = WINDOW + Q_BLOCK
N_KEY_BLOCKS = SPAN // Q_BLOCK
SB_GROUP = 2
SB_NEAR_BLOCKS = 3
EXIT_LOG_WEIGHT = -110.0
ATT_TILE = 512
WIN_TILES = WINDOW // ATT_TILE + 1
CHUNK = 64
N_BUCKETS = 32
MAX_DISTANCE = 128
LRU_C = 8.0
LRU_CONV = 4
SC_CONV = 3
FFN_CONV = 3
HALO = 16
NORM_EPS = 1e-6
NEG_INF = -1e30

(SB_Q, SB_K, SB_V, DF_Q, DF_K, DF_V, LRU_X, LRU_G, SC_B, SC_C, SC_X) = range(N_IN_BLOCKS)


def _params(semantics, vmem_mib):
    return pltpu.CompilerParams(dimension_semantics=semantics, vmem_limit_bytes=vmem_mib << 20)


def _rms(v, gain):
    return v * lax.rsqrt(jnp.mean(v * v, axis=-1, keepdims=True) + NORM_EPS) * gain


def _nt_dot(a, b):
    return lax.dot_general(a, b, (((1,), (1,)), ((), ())), preferred_element_type=F32)


def _inproj_kernel(x_ref, g_ref, w_ref, o_ref, h_ref):
    @pl.when(pl.program_id(1) == 0)
    def _():
        h_ref[...] = _rms(x_ref[...], g_ref[...]).astype(BF16)

    res = jnp.dot(h_ref[...], w_ref[...], preferred_element_type=F32)
    for c in range(o_ref.shape[0]):
        o_ref[c] = res[:, c * LANE:(c + 1) * LANE].astype(BF16)


def _inproj(x2, gain, w, tm=1024, tn=512):
    n, d = x2.shape
    cols = w.shape[1]
    n_rows = n // tm
    return pl.pallas_call(
        _inproj_kernel,
        grid=(n_rows, cols // tn),
        in_specs=[
            pl.BlockSpec((tm, d), lambda i, j: (jnp.minimum(i + jnp.minimum(j, 1), n_rows - 1), 0)),
            pl.BlockSpec((1, d), lambda i, j: (0, 0)),
            pl.BlockSpec((d, tn), lambda i, j: (0, j)),
        ],
        out_specs=pl.BlockSpec((tn // LANE, tm, LANE), lambda i, j: (j, i, 0)),
        out_shape=jax.ShapeDtypeStruct((cols // LANE, n, LANE), BF16),
        scratch_shapes=[pltpu.VMEM((tm, d), BF16)],
        compiler_params=_params(("parallel", "arbitrary"), 40),
        name="inproj",
    )(x2, gain, w)


def _window_specs(group0, n_tiles):
    specs = []
    for back in range(WIN_TILES - 1, -1, -1):
        specs.append(pl.BlockSpec(
            (None, ATT_TILE, LANE),
            lambda b, g, i, back=back: (group0 + g, b * n_tiles + jnp.maximum(i - back, 0), 0)))
    return specs


def _next_tile_spec(group0, n_tiles):
    return pl.BlockSpec(
        (None, ATT_TILE, LANE),
        lambda b, g, i: (group0 + g, b * n_tiles + jnp.minimum(i + 1, n_tiles - 1), 0))


def _fill_window(dst, tiles):
    for t, ref in enumerate(tiles):
        dst[t * ATT_TILE:(t + 1) * ATT_TILE, :] = ref[...]


def _mask_missing_tiles(set_cols):
    i = pl.program_id(2)
    for missing in range(1, WIN_TILES):
        @pl.when(i == WIN_TILES - 1 - missing)
        def _(missing=missing):
            set_cols(missing * ATT_TILE)


def _half_select(lane, half):
    return (lane < HEAD_DIM) if half == 0 else (lane >= HEAD_DIM)


def _key_groups(lo, hi):
    groups = [(max(top - SB_GROUP, lo), top) for top in range(hi, lo, -SB_GROUP)]
    return groups


def _sb_weights(z_of, groups, later, tri_ref, newest_mask, oldest_mask):
    w_parts = []
    for lo, hi in groups:
        width = (hi - lo) * Q_BLOCK
        z = z_of(lo, hi)
        if hi == N_KEY_BLOCKS:
            z = z + newest_mask[:, newest_mask.shape[1] - width:]
        if lo == 0:
            oldest = z[:, :Q_BLOCK] + oldest_mask
            z = oldest if width == Q_BLOCK else jnp.concatenate([oldest, z[:, Q_BLOCK:]], axis=1)
        log_keep = jnp.minimum(-z, 0.0) - jnp.log(1.0 + jnp.exp(-jnp.abs(z)))
        within = jnp.dot(log_keep.astype(BF16), tri_ref[0:width, 0:width],
                         preferred_element_type=F32)
        after = within if later is None else within + jnp.concatenate([later] * (hi - lo), axis=1)
        w_parts.append(jnp.exp(z + log_keep + after).astype(BF16))
        total = jnp.broadcast_to(jnp.sum(log_keep, axis=-1, keepdims=True), (Q_BLOCK, LANE))
        later = total if later is None else later + total
    return jnp.concatenate(w_parts[::-1], axis=1), later


def _sb_near(znear, vwin, tri_ref, newest_mask, acc_s, later_s, keep_s, arg_s, within_s, newer_s, w_s):
    n_q = ATT_TILE // Q_BLOCK
    first_near = N_KEY_BLOCKS - SB_NEAR_BLOCKS
    near_groups = _key_groups(first_near, N_KEY_BLOCKS)
    cols_of = lambda lo, hi: slice((lo - first_near) * Q_BLOCK, (hi - first_near) * Q_BLOCK)
    pairs = [(half, r) for half in range(2) for r in range(n_q)]
    pair_rows = lambda c: slice(c * Q_BLOCK, (c + 1) * Q_BLOCK)

    worst = [None] * n_q
    for c, (half, r) in enumerate(pairs):
        later = None
        for g, (lo, hi) in enumerate(near_groups):
            z = znear[half, r * Q_BLOCK:(r + 1) * Q_BLOCK,
                      (r + lo - first_near) * Q_BLOCK:(r + hi - first_near) * Q_BLOCK]
            if hi == N_KEY_BLOCKS:
                z = z + newest_mask[:, newest_mask.shape[1] - (hi - lo) * Q_BLOCK:]
            log_keep = jnp.minimum(-z, 0.0) - jnp.log(1.0 + jnp.exp(-jnp.abs(z)))
            keep_s[pair_rows(c), cols_of(lo, hi)] = log_keep.astype(BF16)
            arg_s[pair_rows(c), cols_of(lo, hi)] = z + log_keep
            if g > 0:
                newer_s[g - 1, c] = later
            total = jnp.broadcast_to(jnp.sum(log_keep, axis=-1, keepdims=True), (Q_BLOCK, LANE))
            later = total if later is None else later + total
        later_s[c] = later
        worst[r] = later if worst[r] is None else jnp.maximum(worst[r], later)

    for lo, hi in near_groups:
        width = (hi - lo) * Q_BLOCK
        within_s[:, cols_of(lo, hi)] = jnp.dot(keep_s[:, cols_of(lo, hi)], tri_ref[0:width, 0:width],
                                               preferred_element_type=F32)

    for c, (half, r) in enumerate(pairs):
        for g, (lo, hi) in enumerate(near_groups):
            arg = arg_s[pair_rows(c), cols_of(lo, hi)] + within_s[pair_rows(c), cols_of(lo, hi)]
            if g > 0:
                arg = arg + jnp.concatenate([newer_s[g - 1, c]] * (hi - lo), axis=1)
            w_s[r, half * Q_BLOCK:(half + 1) * Q_BLOCK, cols_of(lo, hi)] = jnp.exp(arg).astype(BF16)

    for r in range(n_q):
        out = jnp.dot(w_s[r], vwin[(r + first_near) * Q_BLOCK:(r + N_KEY_BLOCKS) * Q_BLOCK, :],
                      preferred_element_type=F32)
        for half in range(2):
            acc_s[half, r * Q_BLOCK:(r + 1) * Q_BLOCK, :] = out[half * Q_BLOCK:(half + 1) * Q_BLOCK]

    return worst


def _sb_kernel(q_ref, qn_ref, k2_ref, k1_ref, k0_ref, kn_ref, v2_ref, v1_ref, v0_ref, tri_ref, o_ref,
               kwin, knear, vwin, znear_a, znear_b, zfar, acc_s, later_s, *near_scratch):
    i = pl.program_id(2)
    lane = lax.broadcasted_iota(jnp.int32, (1, LANE), 1)
    row = lax.broadcasted_iota(jnp.int32, (Q_BLOCK, SB_GROUP * Q_BLOCK), 0)
    col = lax.broadcasted_iota(jnp.int32, (Q_BLOCK, SB_GROUP * Q_BLOCK), 1)
    newest_mask = jnp.where(col < row + (SB_GROUP - 1) * Q_BLOCK, 0.0, NEG_INF)
    oldest_mask = jnp.where(lax.broadcasted_iota(jnp.int32, (Q_BLOCK, Q_BLOCK), 1)
                            >= lax.broadcasted_iota(jnp.int32, (Q_BLOCK, Q_BLOCK), 0),
                            0.0, NEG_INF)
    n_q = ATT_TILE // Q_BLOCK
    first_near = N_KEY_BLOCKS - SB_NEAR_BLOCKS
    near_row0 = first_near * Q_BLOCK
    older = (WIN_TILES - 1) * ATT_TILE - near_row0

    def masked_queries(ref):
        q2 = ref[...] * (HEAD_DIM ** -0.5)
        return [jnp.where(_half_select(lane, half), q2, jnp.zeros_like(q2)) for half in range(2)]

    @pl.when(i == 0)
    def _():
        for half, qm in enumerate(masked_queries(q_ref)):
            znear_a[half, :, 0:older] = jnp.full((ATT_TILE, older), NEG_INF, F32)
            znear_a[half, :, older:] = _nt_dot(qm, k0_ref[...])

    def step(cur, nxt):
        knear[0:older, :] = k0_ref[ATT_TILE - older:, :]
        knear[older:, :] = kn_ref[...]
        for half, qm in enumerate(masked_queries(qn_ref)):
            nxt[half] = _nt_dot(qm, knear[...])

        _fill_window(vwin, (v2_ref, v1_ref, v0_ref))
        worst_of = _sb_near(cur, vwin, tri_ref, newest_mask, acc_s, later_s, *near_scratch)

        missing_rows = jnp.maximum(WIN_TILES - 1 - i, 0) * ATT_TILE
        worst = None
        for r in range(n_q):
            has_far_keys = (r + first_near) * Q_BLOCK > missing_rows
            worst_r = jnp.where(has_far_keys, worst_of[r], NEG_INF)
            worst = worst_r if worst is None else jnp.maximum(worst, worst_r)

        @pl.when(jnp.max(worst) >= EXIT_LOG_WEIGHT)
        def _():
            _fill_window(kwin, (k2_ref, k1_ref, k0_ref))
            for half, qm in enumerate(masked_queries(q_ref)):
                zfar[...] = _nt_dot(qm, kwin[0:(n_q - 1 + first_near) * Q_BLOCK, :])

                def _set(ncols):
                    zfar[:, 0:ncols] = jnp.full((ATT_TILE, ncols), NEG_INF, F32)
                _mask_missing_tiles(_set)

                for r in range(n_q):
                    rows = slice(r * Q_BLOCK, (r + 1) * Q_BLOCK)
                    z_of = lambda lo, hi, r=r, rows=rows: zfar[
                        rows, (r + lo) * Q_BLOCK:(r + hi) * Q_BLOCK]
                    w, _ = _sb_weights(z_of, _key_groups(0, first_near), later_s[half * n_q + r],
                                       tri_ref, newest_mask, oldest_mask)
                    acc_s[half, rows, :] += jnp.dot(
                        w, vwin[r * Q_BLOCK:(r + first_near) * Q_BLOCK, :],
                        preferred_element_type=F32)

        o_ref[...] = jnp.where(lane < HEAD_DIM, acc_s[0], acc_s[1]).astype(BF16)

    @pl.when(i % 2 == 0)
    def _():
        step(znear_a, znear_b)

    @pl.when(i % 2 == 1)
    def _():
        step(znear_b, znear_a)


def _sb_attention(p, tri, batch, seq):
    n = batch * seq
    n_tiles = seq // ATT_TILE
    n_q = ATT_TILE // Q_BLOCK
    near_row0 = (N_KEY_BLOCKS - SB_NEAR_BLOCKS) * Q_BLOCK
    far_rows = (n_q - 1 + N_KEY_BLOCKS - SB_NEAR_BLOCKS) * Q_BLOCK
    near_cols = SB_NEAR_BLOCKS * Q_BLOCK
    n_near_groups = len(_key_groups(N_KEY_BLOCKS - SB_NEAR_BLOCKS, N_KEY_BLOCKS))
    tile = lambda g0: pl.BlockSpec((None, ATT_TILE, LANE),
                                   lambda b, g, i: (g0 + g, b * n_tiles + i, 0))
    near_rows = WIN_TILES * ATT_TILE - near_row0
    near_logits = pltpu.VMEM((2, ATT_TILE, near_rows), F32)
    return pl.pallas_call(
        _sb_kernel,
        grid=(batch, GROUP_LANES, n_tiles),
        in_specs=[tile(SB_Q * GROUP_LANES), _next_tile_spec(SB_Q * GROUP_LANES, n_tiles)]
        + _window_specs(SB_K * GROUP_LANES, n_tiles) + [_next_tile_spec(SB_K * GROUP_LANES, n_tiles)]
        + _window_specs(SB_V * GROUP_LANES, n_tiles)
        + [pl.BlockSpec(tri.shape, lambda b, g, i: (0, 0))],
        out_specs=tile(0),
        out_shape=jax.ShapeDtypeStruct((GROUP_LANES, n, LANE), BF16),
        scratch_shapes=[pltpu.VMEM((WIN_TILES * ATT_TILE, LANE), BF16),
                        pltpu.VMEM((near_rows, LANE), BF16),
                        pltpu.VMEM((WIN_TILES * ATT_TILE, LANE), BF16),
                        near_logits, near_logits,
                        pltpu.VMEM((ATT_TILE, far_rows), F32),
                        pltpu.VMEM((2, ATT_TILE, LANE), F32),
                        pltpu.VMEM((2 * n_q, Q_BLOCK, LANE), F32),
                        pltpu.VMEM((2 * ATT_TILE, near_cols), BF16),
                        pltpu.VMEM((2 * ATT_TILE, near_cols), F32),
                        pltpu.VMEM((2 * ATT_TILE, near_cols), F32),
                        pltpu.VMEM((n_near_groups - 1, 2 * n_q, Q_BLOCK, LANE), F32),
                        pltpu.VMEM((n_q, 2 * Q_BLOCK, near_cols), BF16)],
        compiler_params=_params(("parallel", "parallel", "arbitrary"), 32),
        name="sb_attention",
    )(*([p] * 9), tri)


def _bias_kernel(rb_ref, bucket_ref, band_ref, o_ref):
    bucket = bucket_ref[...]
    band = band_ref[...] > 0
    for h in range(o_ref.shape[0]):
        acc = jnp.zeros(bucket.shape, F32)
        for b in range(N_BUCKETS):
            acc = jnp.where(bucket == b, rb_ref[b, h], acc)
        o_ref[h] = jnp.where(band, acc, NEG_INF)


def _t5_bucket(rel):
    nb = N_BUCKETS // 2
    max_exact = nb // 2
    ret = jnp.where(rel > 0, nb, 0)
    n = jnp.abs(rel)
    large = max_exact + (jnp.log(jnp.maximum(n, 1).astype(jnp.float32) / max_exact)
                         / math.log(MAX_DISTANCE / max_exact) * (nb - max_exact)).astype(jnp.int32)
    large = jnp.minimum(large, nb - 1)
    return ret + jnp.where(n < max_exact, n, large)


def _diff_bias_table(rel_bias):
    heads = rel_bias.shape[1]
    a_key = jnp.arange(SPAN) - WINDOW
    a_qry = jnp.arange(Q_BLOCK)
    bucket = _t5_bucket(a_key[None, :] - a_qry[:, None]).astype(jnp.int32)
    kc = (a_key // CHUNK)[None, :]
    qc = (a_qry // CHUNK)[:, None]
    band = ((kc <= qc) & (kc >= qc - WINDOW // CHUNK)).astype(jnp.int32)
    return pl.pallas_call(
        _bias_kernel,
        in_specs=[pl.BlockSpec(memory_space=pltpu.SMEM),
                  pl.BlockSpec((Q_BLOCK, SPAN), lambda: (0, 0)),
                  pl.BlockSpec((Q_BLOCK, SPAN), lambda: (0, 0))],
        out_specs=pl.BlockSpec((heads, Q_BLOCK, SPAN), lambda: (0, 0, 0)),
        out_shape=jax.ShapeDtypeStruct((heads, Q_BLOCK, SPAN), F32),
        name="diff_bias_table",
    )(rel_bias.astype(F32), bucket, band)


def _diff_kernel(q_ref, qn_ref, k2_ref, k1_ref, k0_ref, kn_ref, v2_ref, v1_ref, v0_ref,
                 bias_ref, lam_ref, g_ref, o_ref, kwin, vwin, logits_a, logits_b, lam_s, *, lam_init):
    i = pl.program_id(2)
    lane = lax.broadcasted_iota(jnp.int32, (1, LANE), 1)
    gain = g_ref[...] * (1.0 - lam_init)

    @pl.when(i == 0)
    def _():
        lv = lam_ref[...]
        lam0 = (jnp.exp(jnp.sum(lv[0:1] * lv[1:2], axis=-1, keepdims=True))
                - jnp.exp(jnp.sum(lv[2:3] * lv[3:4], axis=-1, keepdims=True)) + lam_init)
        lam_s[...] = jnp.broadcast_to(lam0, lam_s.shape)

    lam = lam_s[0:1, 0:1]

    def masked_queries(ref):
        q2 = ref[...] * (HEAD_DIM ** -0.5)
        return [jnp.where(_half_select(lane, m), q2, jnp.zeros_like(q2)) for m in range(2)]

    @pl.when(i == 0)
    def _():
        _fill_window(kwin, (k2_ref, k1_ref, k0_ref))
        ncols = (WIN_TILES - 1) * ATT_TILE
        for m, qm in enumerate(masked_queries(q_ref)):
            logits_a[m, :, 0:ncols] = jnp.full((ATT_TILE, ncols), NEG_INF, F32)
            logits_a[m, :, ncols:] = _nt_dot(qm, kwin[ncols:, :])

    def step(cur, nxt):
        _fill_window(kwin, (k1_ref, k0_ref, kn_ref))
        for m, qm in enumerate(masked_queries(qn_ref)):
            res = _nt_dot(qm, kwin[...])
            nxt[m, :, 0:ATT_TILE] = jnp.where(i == 0, NEG_INF, res[:, 0:ATT_TILE])
            nxt[m, :, ATT_TILE:] = res[:, ATT_TILE:]

        _fill_window(vwin, (v2_ref, v1_ref, v0_ref))
        bias = bias_ref[...]
        for r in range(ATT_TILE // Q_BLOCK):
            rows = slice(r * Q_BLOCK, (r + 1) * Q_BLOCK)
            probs = []
            for m in range(2):
                logits = cur[m, rows, r * Q_BLOCK:r * Q_BLOCK + SPAN] + bias
                e = jnp.exp(logits - jnp.max(logits, axis=-1, keepdims=True))
                probs.append((e, jnp.sum(e, axis=-1, keepdims=True)))
            attn = probs[0][0] * (1.0 / probs[0][1]) - probs[1][0] * (lam / probs[1][1])
            out = jnp.dot(attn.astype(BF16), vwin[r * Q_BLOCK:r * Q_BLOCK + SPAN, :],
                          preferred_element_type=F32)
            o_ref[rows, :] = _rms(out, gain).astype(BF16)

    @pl.when(i % 2 == 0)
    def _():
        step(logits_a, logits_b)

    @pl.when(i % 2 == 1)
    def _():
        step(logits_b, logits_a)


def _diff_attention(p, bias, lam_params, subln_gain, lam_init, batch, seq):
    n = batch * seq
    n_tiles = seq // ATT_TILE
    heads = bias.shape[0]
    tile = lambda g0: pl.BlockSpec((None, ATT_TILE, LANE),
                                   lambda b, g, i: (g0 + g, b * n_tiles + i, 0))
    next_tile = lambda g0: _next_tile_spec(g0, n_tiles)
    logits = pltpu.VMEM((2, ATT_TILE, WIN_TILES * ATT_TILE), F32)
    return pl.pallas_call(
        functools.partial(_diff_kernel, lam_init=lam_init),
        grid=(batch, heads, n_tiles),
        in_specs=[tile(DF_Q * GROUP_LANES), next_tile(DF_Q * GROUP_LANES)]
        + _window_specs(DF_K * GROUP_LANES, n_tiles) + [next_tile(DF_K * GROUP_LANES)]
        + _window_specs(DF_V * GROUP_LANES, n_tiles)
        + [pl.BlockSpec((None, Q_BLOCK, SPAN), lambda b, g, i: (g, 0, 0)),
           pl.BlockSpec(lam_params.shape, lambda b, g, i: (0, 0)),
           pl.BlockSpec((1, LANE), lambda b, g, i: (0, 0))],
        out_specs=tile(0),
        out_shape=jax.ShapeDtypeStruct((heads, n, LANE), BF16),
        scratch_shapes=[pltpu.VMEM((WIN_TILES * ATT_TILE, LANE), BF16),
                        pltpu.VMEM((WIN_TILES * ATT_TILE, LANE), BF16),
                        logits, logits, pltpu.VMEM((8, LANE), F32)],
        compiler_params=_params(("parallel", "parallel", "arbitrary"), 40),
        name="diff_attention",
    )(*([p] * 9), bias, lam_params, subln_gain)


def _softplus(v):
    return jnp.maximum(v, 0.0) + jnp.log1p(jnp.exp(-jnp.abs(v)))


def _expm1(y):
    u = jnp.exp(y)
    is_one = u == 1.0
    near_zero = jnp.where(is_one, y, (u - 1.0) * y / jnp.where(is_one, 1.0, jnp.log(u)))
    return jnp.where(y > -0.5, near_zero, u - 1.0)


def _causal_conv(padded, taps):
    n_taps = taps.shape[0]
    rows, width = padded.shape
    groups = padded.reshape(rows // 8, 8, width)
    sub = lax.broadcasted_iota(jnp.int32, (1, 8, width), 1)
    out = groups[1:] * taps[n_taps - 1:n_taps]
    for k in range(n_taps - 1):
        back = n_taps - 1 - k
        turned = pltpu.roll(groups, back, 1)
        out = out + jnp.where(sub >= back, turned[1:], turned[:-1]) * taps[k:k + 1]
    return out.reshape(rows - 8, width)


def _recurrent_kernel(x_ref, gate_ref, scb_ref, scc_ref, scx_ref,
                      cw_ref, cb_ref, wg_ref, bg_ref, lam_ref, scw_ref,
                      ylru_ref, ysc_ref,
                      xext, sext, a_s, b_s, h_s, carry):
    t = x_ref.shape[1]

    @pl.when(pl.program_id(1) == 0)
    def _():
        xext[0:HALO, :] = jnp.zeros((HALO, GROUP_W), F32)
        sext[0:HALO, :] = jnp.zeros((HALO, GROUP_W), F32)
        carry[...] = jnp.zeros(carry.shape, F32)

    for c in range(GROUP_LANES):
        lanes = slice(c * LANE, (c + 1) * LANE)
        xext[HALO:, lanes] = x_ref[c].astype(F32)
        sext[HALO:, lanes] = scc_ref[c].astype(F32) * scx_ref[c].astype(F32)

    for c in range(GROUP_LANES):
        lanes = slice(c * LANE, (c + 1) * LANE)
        xc = cb_ref[:, lanes] + _causal_conv(xext[HALO - 8:, lanes], cw_ref[:, lanes])
        gates = jax.nn.sigmoid(
            jnp.dot(xc.astype(BF16), wg_ref[c], preferred_element_type=F32) + bg_ref[c])
        r_gate = gates[:, :LANE]
        i_gate = gates[:, LANE:]
        log_a = (-LRU_C) * r_gate * _softplus(-lam_ref[:, lanes])
        a_s[:, lanes] = jnp.exp(log_a)
        b_s[:, lanes] = jnp.sqrt(-_expm1(2.0 * log_a)) * (i_gate * xc)

        sc = _causal_conv(sext[HALO - 8:, lanes], scw_ref[:, lanes])
        ysc_ref[c] = (scb_ref[c].astype(F32) * sc).astype(BF16)

    xext[0:HALO, :] = xext[t:t + HALO, :]
    sext[0:HALO, :] = sext[t:t + HALO, :]

    sub = lax.broadcasted_iota(jnp.int32, (8, GROUP_W), 0)

    def step(j, h_prev):
        rows = pl.ds(pl.multiple_of(j * 8, 8), 8)
        a = a_s[rows, :]
        b = b_s[rows, :]
        for d in (1, 2, 4):
            keep = sub >= d
            b = a * jnp.where(keep, pltpu.roll(b, d, 0), 0.0) + b
            a = a * jnp.where(keep, pltpu.roll(a, d, 0), 1.0)
        h = a * h_prev + b
        h_s[rows, :] = h
        return jnp.broadcast_to(h[7:8, :], (8, GROUP_W))

    carry[...] = lax.fori_loop(0, t // 8, step, carry[...], unroll=4)

    for c in range(GROUP_LANES):
        lanes = slice(c * LANE, (c + 1) * LANE)
        ylru_ref[c] = (h_s[:, lanes]
                       * jax.nn.gelu(gate_ref[c].astype(F32), approximate=True)).astype(BF16)


def _recurrent(p, conv_w, conv_b, w_gate, b_gate, lam, sc_w, batch, seq, t=512):
    n = batch * seq
    n_tiles = seq // t
    blk = lambda which: pl.BlockSpec((GROUP_LANES, t, LANE),
                                     lambda b, i: (which, b * n_tiles + i, 0))
    full = lambda a: pl.BlockSpec(a.shape, lambda b, i: (0,) * a.ndim)
    small = (conv_w, conv_b, w_gate, b_gate, lam, sc_w)
    out = jax.ShapeDtypeStruct((GROUP_LANES, n, LANE), BF16)
    return pl.pallas_call(
        _recurrent_kernel,
        grid=(batch, n_tiles),
        in_specs=[blk(LRU_X), blk(LRU_G), blk(SC_B), blk(SC_C), blk(SC_X)] + [full(a) for a in small],
        out_specs=[blk(0), blk(0)],
        out_shape=[out, out],
        scratch_shapes=[pltpu.VMEM((t + HALO, GROUP_W), F32),
                        pltpu.VMEM((t + HALO, GROUP_W), F32),
                        pltpu.VMEM((t, GROUP_W), F32),
                        pltpu.VMEM((t, GROUP_W), F32),
                        pltpu.VMEM((t, GROUP_W), F32),
                        pltpu.VMEM((8, GROUP_W), F32)],
        compiler_params=_params(("parallel", "arbitrary"), 32),
        name="recurrent",
    )(p, p, p, p, p, *small)


def _gate_weights(w_gate, b_gate):
    per_group = LANE // w_gate.shape[-1]
    zeros = jnp.zeros(w_gate.shape[-2:], w_gate.dtype)
    groups = []
    for c in range(GROUP_LANES):
        halves = []
        for gate in range(2):
            blocks = [w_gate[gate, c * per_group + j] for j in range(per_group)]
            halves.append(jnp.block([[blocks[a] if a == b else zeros for b in range(per_group)]
                                     for a in range(per_group)]))
        groups.append(jnp.concatenate(halves, axis=1))
    wg = jnp.stack(groups).astype(BF16)
    bg = jnp.concatenate([b_gate[0].reshape(GROUP_LANES, 1, LANE),
                          b_gate[1].reshape(GROUP_LANES, 1, LANE)], axis=-1).astype(F32)
    return wg, bg


def _outproj_kernel(sb_ref, df_ref, lru_ref, sc_ref, w_ref, x_ref, g1_ref, g2_ref,
                    xo_ref, h_ref, mixed):
    for a, ref in enumerate((sb_ref, df_ref, lru_ref, sc_ref)):
        for c in range(GROUP_LANES):
            col = (a * GROUP_LANES + c) * LANE
            mixed[:, col:col + LANE] = ref[c]
    y = jnp.dot(mixed[...], w_ref[...], preferred_element_type=F32)
    x_new = x_ref[...] + _rms(y, g1_ref[...])
    xo_ref[...] = x_new
    h_ref[...] = _rms(x_new, g2_ref[...]).astype(BF16)


def _outproj(y_sb, y_df, y_lru, y_sc, w, x2, g1, g2, tm=512):
    n, d = x2.shape
    blk = pl.BlockSpec((GROUP_LANES, tm, LANE), lambda i: (0, i, 0))
    row = pl.BlockSpec((tm, d), lambda i: (i, 0))
    vec = pl.BlockSpec((1, d), lambda i: (0, 0))
    return pl.pallas_call(
        _outproj_kernel,
        grid=(n // tm,),
        in_specs=[blk, blk, blk, blk, pl.BlockSpec(w.shape, lambda i: (0, 0)), row, vec, vec],
        out_specs=[row, row],
        out_shape=[jax.ShapeDtypeStruct((n, d), F32), jax.ShapeDtypeStruct((n, d), BF16)],
        scratch_shapes=[pltpu.VMEM((tm, w.shape[0]), BF16)],
        compiler_params=_params(("parallel",), 48),
        name="outproj",
    )(y_sb, y_df, y_lru, y_sc, w, x2, g1, g2)


def _ffn_up_kernel(h_ref, wg_ref, wu_ref, cg_ref, cu_ref, o_ref, ug_s, uu_s, carry_g, carry_u,
                   *, tiles_per_seq):
    i = pl.program_id(0)
    j = pl.program_id(1)
    tm, tf = o_ref.shape
    @pl.when(i == 0)
    def _():
        carry_g[j] = jnp.zeros(carry_g.shape[1:], F32)
        carry_u[j] = jnp.zeros(carry_u.shape[1:], F32)

    first = (i % tiles_per_seq) == 0
    ug_s[0:8, :] = jnp.where(first, 0.0, carry_g[j])
    uu_s[0:8, :] = jnp.where(first, 0.0, carry_u[j])

    n_blocks = tm // FFN_ROWS

    def project(c, k):
        cols = slice(c * MXU_COLS, (c + 1) * MXU_COLS)
        rb = k % n_blocks
        w_ref, dst_ref = ((wg_ref, ug_s), (wu_ref, uu_s))[k // n_blocks]
        dst_ref[8 + rb * FFN_ROWS:8 + (rb + 1) * FFN_ROWS, cols] = jnp.dot(
            h_ref[rb * FFN_ROWS:(rb + 1) * FFN_ROWS, :], w_ref[:, cols], preferred_element_type=F32)

    strips_per_dot = tm // (2 * n_blocks * FFN_STRIP)

    def activate(c, k):
        cols = slice(c * MXU_COLS, (c + 1) * MXU_COLS)
        src = slice(k * FFN_STRIP, (k + 1) * FFN_STRIP + 8)
        gate = _causal_conv(ug_s[src, cols], cg_ref[:, cols])
        up = _causal_conv(uu_s[src, cols], cu_ref[:, cols])
        o_ref[k * FFN_STRIP:(k + 1) * FFN_STRIP, cols] = (
            jax.nn.gelu(gate, approximate=True) * up).astype(BF16)

    n_chunks = tf // MXU_COLS
    for c in range(n_chunks + 1):
        for k in range(2 * n_blocks):
            if c < n_chunks:
                project(c, k)
            if c > 0:
                for s in range(k * strips_per_dot, (k + 1) * strips_per_dot):
                    activate(c - 1, s)
    carry_g[j] = ug_s[tm:tm + 8, :]
    carry_u[j] = uu_s[tm:tm + 8, :]


def _ffn_up(h2, w_up, conv_w, seq, tm=1024, tf=1024):
    n, d = h2.shape
    d_ff = w_up.shape[1] // 2
    n_f = d_ff // tf
    assert seq % tm == 0 and d_ff % tf == 0
    return pl.pallas_call(
        functools.partial(_ffn_up_kernel, tiles_per_seq=seq // tm),
        grid=(n // tm, n_f),
        in_specs=[
            pl.BlockSpec((tm, d), lambda i, j: (i, 0)),
            pl.BlockSpec((d, tf), lambda i, j: (0, j)),
            pl.BlockSpec((d, tf), lambda i, j: (0, n_f + j)),
            pl.BlockSpec((FFN_CONV, tf), lambda i, j: (0, j)),
            pl.BlockSpec((FFN_CONV, tf), lambda i, j: (0, n_f + j)),
        ],
        out_specs=pl.BlockSpec((tm, tf), lambda i, j: (i, j)),
        out_shape=jax.ShapeDtypeStruct((n, d_ff), BF16),
        scratch_shapes=[pltpu.VMEM((8 + tm, tf), F32), pltpu.VMEM((8 + tm, tf), F32),
                        pltpu.VMEM((n_f, 8, tf), F32), pltpu.VMEM((n_f, 8, tf), F32)],
        compiler_params=_params(("arbitrary", "arbitrary"), 48),
        name="ffn_up",
    )(h2, w_up, w_up, conv_w, conv_w)


def _ffn_down_kernel(a_ref, w_ref, x_ref, g_ref, o_ref):
    y = jnp.dot(a_ref[...], w_ref[...], preferred_element_type=F32)
    o_ref[...] = x_ref[...] + _rms(y, g_ref[...])


def _ffn_down(act, w_down, x2, gain, tm=512):
    n, d = x2.shape
    row = pl.BlockSpec((tm, d), lambda i: (i, 0))
    return pl.pallas_call(
        _ffn_down_kernel,
        grid=(n // tm,),
        in_specs=[
            pl.BlockSpec((tm, act.shape[1]), lambda i: (i, 0)),
            pl.BlockSpec(w_down.shape, lambda i: (0, 0), pipeline_mode=pl.Buffered(1)),
            row,
            pl.BlockSpec((1, d), lambda i: (0, 0)),
        ],
        out_specs=row,
        out_shape=jax.ShapeDtypeStruct((n, d), F32),
        compiler_params=_params(("parallel",), 48),
        name="ffn_down",
    )(act, w_down, x2, gain)


def _later_matrix():
    idx = np.arange(SB_GROUP * Q_BLOCK)
    return jnp.asarray(idx[:, None] > idx[None, :], BF16)


def kernel(x, norm_gains, w_in, w_out, rel_bias, diff_lambda, diff_subln_g, lru_conv_w, lru_conv_b,
           lru_w_gate, lru_b_gate, lru_lambda, sc_conv_w, ffn_w_up, ffn_conv_w, ffn_w_down):
    batch, seq, d = x.shape
    depth = w_in.shape[0]
    assert seq % ATT_TILE == 0 and w_in.shape[2] == N_IN_BLOCKS * GROUP_W
    x2 = x.reshape(batch * seq, d).astype(F32)
    tri = _later_matrix()
    bias = _diff_bias_table(rel_bias)
    row = lambda v: v.reshape(1, -1).astype(F32)
    for l in range(depth):
        lam_init = 0.8 - 0.6 * math.exp(-0.3 * l)
        p = _inproj(x2, row(norm_gains[l, 0]), w_in[l].astype(BF16))
        y_sb = _sb_attention(p, tri, batch, seq)
        y_df = _diff_attention(p, bias, diff_lambda[l].astype(F32), row(diff_subln_g[l]),
                               lam_init, batch, seq)
        wg, bg = _gate_weights(lru_w_gate[l], lru_b_gate[l])
        y_lru, y_sc = _recurrent(p, lru_conv_w[l].astype(F32), row(lru_conv_b[l]), wg, bg,
                                 row(lru_lambda[l]), sc_conv_w[l].astype(F32), batch, seq)
        x2, h2 = _outproj(y_sb, y_df, y_lru, y_sc, w_out[l].astype(BF16), x2,
                          row(norm_gains[l, 1]), row(norm_gains[l, 2]))
        act = _ffn_up(h2, ffn_w_up[l].astype(BF16), ffn_conv_w[l].astype(F32), seq)
        x2 = _ffn_down(act, ffn_w_down[l].astype(BF16), x2, row(norm_gains[l, 3]))
    return x2.reshape(batch, seq, d).astype(x.dtype)
```

```python
import functools
import math

import numpy as np
import jax
import jax.numpy as jnp
from jax import lax
from jax.experimental import pallas as pl
from jax.experimental.pallas import tpu as pltpu

F32 = jnp.float32
BF16 = jnp.bfloat16

LANE = 128
MXU_COLS = 256
FFN_STRIP = 256
FFN_ROWS = 512
HEAD_DIM = 64
GROUP_W = 512
GROUP_LANES = GROUP_W // LANE
N_IN_BLOCKS = 11
Q_BLOCK = 128
WINDOW = 1024
SPAN = WINDOW + Q_BLOCK
N_KEY_BLOCKS = SPAN // Q_BLOCK
SB_GROUP = 2
SB_NEAR_BLOCKS = 3
EXIT_LOG_WEIGHT = -110.0
ATT_TILE = 512
WIN_TILES = WINDOW // ATT_TILE + 1
CHUNK = 64
N_BUCKETS = 32
MAX_DISTANCE = 128
LRU_C = 8.0
LRU_CONV = 4
SC_CONV = 3
FFN_CONV = 3
CAST_BLOCK_BYTES = 4 << 20
HALO = 16
NORM_EPS = 1e-6
NEG_INF = -1e30

(SB_Q, SB_K, SB_V, DF_Q, DF_K, DF_V, LRU_X, LRU_G, SC_B, SC_C, SC_X) = range(N_IN_BLOCKS)


def _params(semantics, vmem_mib):
    return pltpu.CompilerParams(dimension_semantics=semantics, vmem_limit_bytes=vmem_mib << 20)


def _rms(v, gain):
    return v * lax.rsqrt(jnp.mean(v * v, axis=-1, keepdims=True) + NORM_EPS) * gain


def _nt_dot(a, b):
    return lax.dot_general(a, b, (((1,), (1,)), ((), ())), preferred_element_type=F32)


def _cast_kernel(w_ref, o_ref):
    o_ref[...] = w_ref[...].astype(BF16)


def _layer_bf16(w, layer):
    _, rows, cols = w.shape
    rb = 1 << ((CAST_BLOCK_BYTES // (4 * cols)).bit_length() - 1)
    assert rows % rb == 0 and rb % 16 == 0
    return pl.pallas_call(
        _cast_kernel,
        grid=(rows // rb,),
        in_specs=[pl.BlockSpec((None, rb, cols), lambda i: (layer, i, 0))],
        out_specs=pl.BlockSpec((rb, cols), lambda i: (i, 0)),
        out_shape=jax.ShapeDtypeStruct((rows, cols), BF16),
        compiler_params=_params(("parallel",), 32),
        name="weight_bf16",
    )(w)


def _inproj_kernel(x_ref, g_ref, w_ref, o_ref, h_ref):
    @pl.when(pl.program_id(1) == 0)
    def _():
        h_ref[...] = _rms(x_ref[...], g_ref[...]).astype(BF16)

    res = jnp.dot(h_ref[...], w_ref[...], preferred_element_type=F32)
    for c in range(o_ref.shape[0]):
        o_ref[c] = res[:, c * LANE:(c + 1) * LANE].astype(BF16)


def _inproj(x2, gain, w, tm=1024, tn=512):
    n, d = x2.shape
    cols = w.shape[1]
    n_rows = n // tm
    return pl.pallas_call(
        _inproj_kernel,
        grid=(n_rows, cols // tn),
        in_specs=[
            pl.BlockSpec((tm, d), lambda i, j: (jnp.minimum(i + jnp.minimum(j, 1), n_rows - 1), 0)),
            pl.BlockSpec((1, d), lambda i, j: (0, 0)),
            pl.BlockSpec((d, tn), lambda i, j: (0, j)),
        ],
        out_specs=pl.BlockSpec((tn // LANE, tm, LANE), lambda i, j: (j, i, 0)),
        out_shape=jax.ShapeDtypeStruct((cols // LANE, n, LANE), BF16),
        scratch_shapes=[pltpu.VMEM((tm, d), BF16)],
        compiler_params=_params(("parallel", "arbitrary"), 40),
        name="inproj",
    )(x2, gain, w)


def _window_specs(group0, n_tiles):
    specs = []
    for back in range(WIN_TILES - 1, -1, -1):
        specs.append(pl.BlockSpec(
            (None, ATT_TILE, LANE),
            lambda b, g, i, back=back: (group0 + g, b * n_tiles + jnp.maximum(i - back, 0), 0)))
    return specs


def _next_tile_spec(group0, n_tiles):
    return pl.BlockSpec(
        (None, ATT_TILE, LANE),
        lambda b, g, i: (group0 + g, b * n_tiles + jnp.minimum(i + 1, n_tiles - 1), 0))


def _fill_window(dst, tiles):
    for t, ref in enumerate(tiles):
        dst[t * ATT_TILE:(t + 1) * ATT_TILE, :] = ref[...]


def _mask_missing_tiles(set_cols):
    i = pl.program_id(2)
    for missing in range(1, WIN_TILES):
        @pl.when(i == WIN_TILES - 1 - missing)
        def _(missing=missing):
            set_cols(missing * ATT_TILE)


def _half_select(lane, half):
    return (lane < HEAD_DIM) if half == 0 else (lane >= HEAD_DIM)


def _key_groups(lo, hi):
    groups = [(max(top - SB_GROUP, lo), top) for top in range(hi, lo, -SB_GROUP)]
    return groups


def _sb_weights(z_of, groups, later, tri_ref, newest_mask, oldest_mask):
    w_parts = []
    for lo, hi in groups:
        width = (hi - lo) * Q_BLOCK
        z = z_of(lo, hi)
        if hi == N_KEY_BLOCKS:
            z = z + newest_mask[:, newest_mask.shape[1] - width:]
        if lo == 0:
            oldest = z[:, :Q_BLOCK] + oldest_mask
            z = oldest if width == Q_BLOCK else jnp.concatenate([oldest, z[:, Q_BLOCK:]], axis=1)
        log_keep = jnp.minimum(-z, 0.0) - jnp.log(1.0 + jnp.exp(-jnp.abs(z)))
        within = jnp.dot(log_keep.astype(BF16), tri_ref[0:width, 0:width],
                         preferred_element_type=F32)
        after = within if later is None else within + jnp.concatenate([later] * (hi - lo), axis=1)
        w_parts.append(jnp.exp(z + log_keep + after).astype(BF16))
        total = jnp.broadcast_to(jnp.sum(log_keep, axis=-1, keepdims=True), (Q_BLOCK, LANE))
        later = total if later is None else later + total
    return jnp.concatenate(w_parts[::-1], axis=1), later


def _sb_near(znear, vwin, tri_ref, newest_mask, acc_s, later_s, keep_s, arg_s, within_s, newer_s, w_s):
    n_q = ATT_TILE // Q_BLOCK
    first_near = N_KEY_BLOCKS - SB_NEAR_BLOCKS
    near_groups = _key_groups(first_near, N_KEY_BLOCKS)
    cols_of = lambda lo, hi: slice((lo - first_near) * Q_BLOCK, (hi - first_near) * Q_BLOCK)
    pairs = [(half, r) for half in range(2) for r in range(n_q)]
    pair_rows = lambda c: slice(c * Q_BLOCK, (c + 1) * Q_BLOCK)

    worst = [None] * n_q
    for c, (half, r) in enumerate(pairs):
        later = None
        for g, (lo, hi) in enumerate(near_groups):
            z = znear[half, r * Q_BLOCK:(r + 1) * Q_BLOCK,
                      (r + lo - first_near) * Q_BLOCK:(r + hi - first_near) * Q_BLOCK]
            if hi == N_KEY_BLOCKS:
                z = z + newest_mask[:, newest_mask.shape[1] - (hi - lo) * Q_BLOCK:]
            log_keep = jnp.minimum(-z, 0.0) - jnp.log(1.0 + jnp.exp(-jnp.abs(z)))
            keep_s[pair_rows(c), cols_of(lo, hi)] = log_keep.astype(BF16)
            arg_s[pair_rows(c), cols_of(lo, hi)] = z + log_keep
            if g > 0:
                newer_s[g - 1, c] = later
            total = jnp.broadcast_to(jnp.sum(log_keep, axis=-1, keepdims=True), (Q_BLOCK, LANE))
            later = total if later is None else later + total
        later_s[c] = later
        worst[r] = later if worst[r] is None else jnp.maximum(worst[r], later)

    for lo, hi in near_groups:
        width = (hi - lo) * Q_BLOCK
        within_s[:, cols_of(lo, hi)] = jnp.dot(keep_s[:, cols_of(lo, hi)], tri_ref[0:width, 0:width],
                                               preferred_element_type=F32)

    for c, (half, r) in enumerate(pairs):
        for g, (lo, hi) in enumerate(near_groups):
            arg = arg_s[pair_rows(c), cols_of(lo, hi)] + within_s[pair_rows(c), cols_of(lo, hi)]
            if g > 0:
                arg = arg + jnp.concatenate([newer_s[g - 1, c]] * (hi - lo), axis=1)
            w_s[r, half * Q_BLOCK:(half + 1) * Q_BLOCK, cols_of(lo, hi)] = jnp.exp(arg).astype(BF16)

    for r in range(n_q):
        out = jnp.dot(w_s[r], vwin[(r + first_near) * Q_BLOCK:(r + N_KEY_BLOCKS) * Q_BLOCK, :],
                      preferred_element_type=F32)
        for half in range(2):
            acc_s[half, r * Q_BLOCK:(r + 1) * Q_BLOCK, :] = out[half * Q_BLOCK:(half + 1) * Q_BLOCK]

    return worst


def _sb_kernel(q_ref, qn_ref, k2_ref, k1_ref, k0_ref, kn_ref, v2_ref, v1_ref, v0_ref, tri_ref, o_ref,
               kwin, knear, vwin, znear_a, znear_b, zfar, acc_s, later_s, *near_scratch):
    i = pl.program_id(2)
    lane = lax.broadcasted_iota(jnp.int32, (1, LANE), 1)
    row = lax.broadcasted_iota(jnp.int32, (Q_BLOCK, SB_GROUP * Q_BLOCK), 0)
    col = lax.broadcasted_iota(jnp.int32, (Q_BLOCK, SB_GROUP * Q_BLOCK), 1)
    newest_mask = jnp.where(col < row + (SB_GROUP - 1) * Q_BLOCK, 0.0, NEG_INF)
    oldest_mask = jnp.where(lax.broadcasted_iota(jnp.int32, (Q_BLOCK, Q_BLOCK), 1)
                            >= lax.broadcasted_iota(jnp.int32, (Q_BLOCK, Q_BLOCK), 0),
                            0.0, NEG_INF)
    n_q = ATT_TILE // Q_BLOCK
    first_near = N_KEY_BLOCKS - SB_NEAR_BLOCKS
    near_row0 = first_near * Q_BLOCK
    older = (WIN_TILES - 1) * ATT_TILE - near_row0

    def masked_queries(ref):
        q2 = ref[...] * (HEAD_DIM ** -0.5)
        return [jnp.where(_half_select(lane, half), q2, jnp.zeros_like(q2)) for half in range(2)]

    @pl.when(i == 0)
    def _():
        for half, qm in enumerate(masked_queries(q_ref)):
            znear_a[half, :, 0:older] = jnp.full((ATT_TILE, older), NEG_INF, F32)
            znear_a[half, :, older:] = _nt_dot(qm, k0_ref[...])

    def step(cur, nxt):
        knear[0:older, :] = k0_ref[ATT_TILE - older:, :]
        knear[older:, :] = kn_ref[...]
        for half, qm in enumerate(masked_queries(qn_ref)):
            nxt[half] = _nt_dot(qm, knear[...])

        _fill_window(vwin, (v2_ref, v1_ref, v0_ref))
        worst_of = _sb_near(cur, vwin, tri_ref, newest_mask, acc_s, later_s, *near_scratch)

        missing_rows = jnp.maximum(WIN_TILES - 1 - i, 0) * ATT_TILE
        worst = None
        for r in range(n_q):
            has_far_keys = (r + first_near) * Q_BLOCK > missing_rows
            worst_r = jnp.where(has_far_keys, worst_of[r], NEG_INF)
            worst = worst_r if worst is None else jnp.maximum(worst, worst_r)

        @pl.when(jnp.max(worst) >= EXIT_LOG_WEIGHT)
        def _():
            _fill_window(kwin, (k2_ref, k1_ref, k0_ref))
            for half, qm in enumerate(masked_queries(q_ref)):
                zfar[...] = _nt_dot(qm, kwin[0:(n_q - 1 + first_near) * Q_BLOCK, :])

                def _set(ncols):
                    zfar[:, 0:ncols] = jnp.full((ATT_TILE, ncols), NEG_INF, F32)
                _mask_missing_tiles(_set)

                for r in range(n_q):
                    rows = slice(r * Q_BLOCK, (r + 1) * Q_BLOCK)
                    z_of = lambda lo, hi, r=r, rows=rows: zfar[
                        rows, (r + lo) * Q_BLOCK:(r + hi) * Q_BLOCK]
                    w, _ = _sb_weights(z_of, _key_groups(0, first_near), later_s[half * n_q + r],
                                       tri_ref, newest_mask, oldest_mask)
                    acc_s[half, rows, :] += jnp.dot(
                        w, vwin[r * Q_BLOCK:(r + first_near) * Q_BLOCK, :],
                        preferred_element_type=F32)

        o_ref[...] = jnp.where(lane < HEAD_DIM, acc_s[0], acc_s[1]).astype(BF16)

    @pl.when(i % 2 == 0)
    def _():
        step(znear_a, znear_b)

    @pl.when(i % 2 == 1)
    def _():
        step(znear_b, znear_a)


def _sb_attention(p, tri, batch, seq):
    n = batch * seq
    n_tiles = seq // ATT_TILE
    n_q = ATT_TILE // Q_BLOCK
    near_row0 = (N_KEY_BLOCKS - SB_NEAR_BLOCKS) * Q_BLOCK
    far_rows = (n_q - 1 + N_KEY_BLOCKS - SB_NEAR_BLOCKS) * Q_BLOCK
    near_cols = SB_NEAR_BLOCKS * Q_BLOCK
    n_near_groups = len(_key_groups(N_KEY_BLOCKS - SB_NEAR_BLOCKS, N_KEY_BLOCKS))
    tile = lambda g0: pl.BlockSpec((None, ATT_TILE, LANE),
                                   lambda b, g, i: (g0 + g, b * n_tiles + i, 0))
    near_rows = WIN_TILES * ATT_TILE - near_row0
    near_logits = pltpu.VMEM((2, ATT_TILE, near_rows), F32)
    return pl.pallas_call(
        _sb_kernel,
        grid=(batch, GROUP_LANES, n_tiles),
        in_specs=[tile(SB_Q * GROUP_LANES), _next_tile_spec(SB_Q * GROUP_LANES, n_tiles)]
        + _window_specs(SB_K * GROUP_LANES, n_tiles) + [_next_tile_spec(SB_K * GROUP_LANES, n_tiles)]
        + _window_specs(SB_V * GROUP_LANES, n_tiles)
        + [pl.BlockSpec(tri.shape, lambda b, g, i: (0, 0))],
        out_specs=tile(0),
        out_shape=jax.ShapeDtypeStruct((GROUP_LANES, n, LANE), BF16),
        scratch_shapes=[pltpu.VMEM((WIN_TILES * ATT_TILE, LANE), BF16),
                        pltpu.VMEM((near_rows, LANE), BF16),
                        pltpu.VMEM((WIN_TILES * ATT_TILE, LANE), BF16),
                        near_logits, near_logits,
                        pltpu.VMEM((ATT_TILE, far_rows), F32),
                        pltpu.VMEM((2, ATT_TILE, LANE), F32),
                        pltpu.VMEM((2 * n_q, Q_BLOCK, LANE), F32),
                        pltpu.VMEM((2 * ATT_TILE, near_cols), BF16),
                        pltpu.VMEM((2 * ATT_TILE, near_cols), F32),
                        pltpu.VMEM((2 * ATT_TILE, near_cols), F32),
                        pltpu.VMEM((n_near_groups - 1, 2 * n_q, Q_BLOCK, LANE), F32),
                        pltpu.VMEM((n_q, 2 * Q_BLOCK, near_cols), BF16)],
        compiler_params=_params(("parallel", "parallel", "arbitrary"), 32),
        name="sb_attention",
    )(*([p] * 9), tri)


def _bias_kernel(rb_ref, bucket_ref, band_ref, o_ref):
    bucket = bucket_ref[...]
    band = band_ref[...] > 0
    for h in range(o_ref.shape[0]):
        acc = jnp.zeros(bucket.shape, F32)
        for b in range(N_BUCKETS):
            acc = jnp.where(bucket == b, rb_ref[b, h], acc)
        o_ref[h] = jnp.where(band, acc, NEG_INF)


def _t5_bucket(rel):
    nb = N_BUCKETS // 2
    max_exact = nb // 2
    ret = jnp.where(rel > 0, nb, 0)
    n = jnp.abs(rel)
    large = max_exact + (jnp.log(jnp.maximum(n, 1).astype(jnp.float32) / max_exact)
                         / math.log(MAX_DISTANCE / max_exact) * (nb - max_exact)).astype(jnp.int32)
    large = jnp.minimum(large, nb - 1)
    return ret + jnp.where(n < max_exact, n, large)


def _diff_bias_table(rel_bias):
    heads = rel_bias.shape[1]
    a_key = jnp.arange(SPAN) - WINDOW
    a_qry = jnp.arange(Q_BLOCK)
    bucket = _t5_bucket(a_key[None, :] - a_qry[:, None]).astype(jnp.int32)
    kc = (a_key // CHUNK)[None, :]
    qc = (a_qry // CHUNK)[:, None]
    band = ((kc <= qc) & (kc >= qc - WINDOW // CHUNK)).astype(jnp.int32)
    return pl.pallas_call(
        _bias_kernel,
        in_specs=[pl.BlockSpec(memory_space=pltpu.SMEM),
                  pl.BlockSpec((Q_BLOCK, SPAN), lambda: (0, 0)),
                  pl.BlockSpec((Q_BLOCK, SPAN), lambda: (0, 0))],
        out_specs=pl.BlockSpec((heads, Q_BLOCK, SPAN), lambda: (0, 0, 0)),
        out_shape=jax.ShapeDtypeStruct((heads, Q_BLOCK, SPAN), F32),
        name="diff_bias_table",
    )(rel_bias.astype(F32), bucket, band)


def _diff_kernel(q_ref, qn_ref, k2_ref, k1_ref, k0_ref, kn_ref, v2_ref, v1_ref, v0_ref,
                 bias_ref, lam_ref, g_ref, o_ref, kwin, vwin, logits_a, logits_b, lam_s, *, lam_init):
    i = pl.program_id(2)
    lane = lax.broadcasted_iota(jnp.int32, (1, LANE), 1)
    gain = g_ref[...] * (1.0 - lam_init)

    @pl.when(i == 0)
    def _():
        lv = lam_ref[...]
        lam0 = (jnp.exp(jnp.sum(lv[0:1] * lv[1:2], axis=-1, keepdims=True))
                - jnp.exp(jnp.sum(lv[2:3] * lv[3:4], axis=-1, keepdims=True)) + lam_init)
        lam_s[...] = jnp.broadcast_to(lam0, lam_s.shape)

    lam = lam_s[0:1, 0:1]

    def masked_queries(ref):
        q2 = ref[...] * (HEAD_DIM ** -0.5)
        return [jnp.where(_half_select(lane, m), q2, jnp.zeros_like(q2)) for m in range(2)]

    @pl.when(i == 0)
    def _():
        _fill_window(kwin, (k2_ref, k1_ref, k0_ref))
        ncols = (WIN_TILES - 1) * ATT_TILE
        for m, qm in enumerate(masked_queries(q_ref)):
            logits_a[m, :, 0:ncols] = jnp.full((ATT_TILE, ncols), NEG_INF, F32)
            logits_a[m, :, ncols:] = _nt_dot(qm, kwin[ncols:, :])

    def step(cur, nxt):
        _fill_window(kwin, (k1_ref, k0_ref, kn_ref))
        for m, qm in enumerate(masked_queries(qn_ref)):
            res = _nt_dot(qm, kwin[...])
            nxt[m, :, 0:ATT_TILE] = jnp.where(i == 0, NEG_INF, res[:, 0:ATT_TILE])
            nxt[m, :, ATT_TILE:] = res[:, ATT_TILE:]

        _fill_window(vwin, (v2_ref, v1_ref, v0_ref))
        bias = bias_ref[...]
        for r in range(ATT_TILE // Q_BLOCK):
            rows = slice(r * Q_BLOCK, (r + 1) * Q_BLOCK)
            probs = []
            for m in range(2):
                logits = cur[m, rows, r * Q_BLOCK:r * Q_BLOCK + SPAN] + bias
                e = jnp.exp(logits - jnp.max(logits, axis=-1, keepdims=True))
                probs.append((e, jnp.sum(e, axis=-1, keepdims=True)))
            attn = probs[0][0] * (1.0 / probs[0][1]) - probs[1][0] * (lam / probs[1][1])
            out = jnp.dot(attn.astype(BF16), vwin[r * Q_BLOCK:r * Q_BLOCK + SPAN, :],
                          preferred_element_type=F32)
            o_ref[rows, :] = _rms(out, gain).astype(BF16)

    @pl.when(i % 2 == 0)
    def _():
        step(logits_a, logits_b)

    @pl.when(i % 2 == 1)
    def _():
        step(logits_b, logits_a)


def _diff_attention(p, bias, lam_params, subln_gain, lam_init, batch, seq):
    n = batch * seq
    n_tiles = seq // ATT_TILE
    heads = bias.shape[0]
    tile = lambda g0: pl.BlockSpec((None, ATT_TILE, LANE),
                                   lambda b, g, i: (g0 + g, b * n_tiles + i, 0))
    next_tile = lambda g0: _next_tile_spec(g0, n_tiles)
    logits = pltpu.VMEM((2, ATT_TILE, WIN_TILES * ATT_TILE), F32)
    return pl.pallas_call(
        functools.partial(_diff_kernel, lam_init=lam_init),
        grid=(batch, heads, n_tiles),
        in_specs=[tile(DF_Q * GROUP_LANES), next_tile(DF_Q * GROUP_LANES)]
        + _window_specs(DF_K * GROUP_LANES, n_tiles) + [next_tile(DF_K * GROUP_LANES)]
        + _window_specs(DF_V * GROUP_LANES, n_tiles)
        + [pl.BlockSpec((None, Q_BLOCK, SPAN), lambda b, g, i: (g, 0, 0)),
           pl.BlockSpec(lam_params.shape, lambda b, g, i: (0, 0)),
           pl.BlockSpec((1, LANE), lambda b, g, i: (0, 0))],
        out_specs=tile(0),
        out_shape=jax.ShapeDtypeStruct((heads, n, LANE), BF16),
        scratch_shapes=[pltpu.VMEM((WIN_TILES * ATT_TILE, LANE), BF16),
                        pltpu.VMEM((WIN_TILES * ATT_TILE, LANE), BF16),
                        logits, logits, pltpu.VMEM((8, LANE), F32)],
        compiler_params=_params(("parallel", "parallel", "arbitrary"), 40),
        name="diff_attention",
    )(*([p] * 9), bias, lam_params, subln_gain)


def _softplus(v):
    return jnp.maximum(v, 0.0) + jnp.log1p(jnp.exp(-jnp.abs(v)))


def _expm1(y):
    u = jnp.exp(y)
    is_one = u == 1.0
    near_zero = jnp.where(is_one, y, (u - 1.0) * y / jnp.where(is_one, 1.0, jnp.log(u)))
    return jnp.where(y > -0.5, near_zero, u - 1.0)


def _causal_conv(padded, taps):
    n_taps = taps.shape[0]
    rows, width = padded.shape
    groups = padded.reshape(rows // 8, 8, width)
    sub = lax.broadcasted_iota(jnp.int32, (1, 8, width), 1)
    out = groups[1:] * taps[n_taps - 1:n_taps]
    for k in range(n_taps - 1):
        back = n_taps - 1 - k
        turned = pltpu.roll(groups, back, 1)
        out = out + jnp.where(sub >= back, turned[1:], turned[:-1]) * taps[k:k + 1]
    return out.reshape(rows - 8, width)


def _recurrent_kernel(x_ref, gate_ref, scb_ref, scc_ref, scx_ref,
                      cw_ref, cb_ref, wg_ref, bg_ref, lam_ref, scw_ref,
                      ylru_ref, ysc_ref,
                      xext, sext, a_s, b_s, h_s, carry):
    t = x_ref.shape[1]

    @pl.when(pl.program_id(1) == 0)
    def _():
        xext[0:HALO, :] = jnp.zeros((HALO, GROUP_W), F32)
        sext[0:HALO, :] = jnp.zeros((HALO, GROUP_W), F32)
        carry[...] = jnp.zeros(carry.shape, F32)

    for c in range(GROUP_LANES):
        lanes = slice(c * LANE, (c + 1) * LANE)
        xext[HALO:, lanes] = x_ref[c].astype(F32)
        sext[HALO:, lanes] = scc_ref[c].astype(F32) * scx_ref[c].astype(F32)

    for c in range(GROUP_LANES):
        lanes = slice(c * LANE, (c + 1) * LANE)
        xc = cb_ref[:, lanes] + _causal_conv(xext[HALO - 8:, lanes], cw_ref[:, lanes])
        gates = jax.nn.sigmoid(
            jnp.dot(xc.astype(BF16), wg_ref[c], preferred_element_type=F32) + bg_ref[c])
        r_gate = gates[:, :LANE]
        i_gate = gates[:, LANE:]
        log_a = (-LRU_C) * r_gate * _softplus(-lam_ref[:, lanes])
        a_s[:, lanes] = jnp.exp(log_a)
        b_s[:, lanes] = jnp.sqrt(-_expm1(2.0 * log_a)) * (i_gate * xc)

        sc = _causal_conv(sext[HALO - 8:, lanes], scw_ref[:, lanes])
        ysc_ref[c] = (scb_ref[c].astype(F32) * sc).astype(BF16)

    xext[0:HALO, :] = xext[t:t + HALO, :]
    sext[0:HALO, :] = sext[t:t + HALO, :]

    sub = lax.broadcasted_iota(jnp.int32, (8, GROUP_W), 0)

    def step(j, h_prev):
        rows = pl.ds(pl.multiple_of(j * 8, 8), 8)
        a = a_s[rows, :]
        b = b_s[rows, :]
        for d in (1, 2, 4):
            keep = sub >= d
            b = a * jnp.where(keep, pltpu.roll(b, d, 0), 0.0) + b
            a = a * jnp.where(keep, pltpu.roll(a, d, 0), 1.0)
        h = a * h_prev + b
        h_s[rows, :] = h
        return jnp.broadcast_to(h[7:8, :], (8, GROUP_W))

    carry[...] = lax.fori_loop(0, t // 8, step, carry[...], unroll=4)

    for c in range(GROUP_LANES):
        lanes = slice(c * LANE, (c + 1) * LANE)
        ylru_ref[c] = (h_s[:, lanes]
                       * jax.nn.gelu(gate_ref[c].astype(F32), approximate=True)).astype(BF16)


def _recurrent(p, conv_w, conv_b, w_gate, b_gate, lam, sc_w, batch, seq, t=512):
    n = batch * seq
    n_tiles = seq // t
    blk = lambda which: pl.BlockSpec((GROUP_LANES, t, LANE),
                                     lambda b, i: (which, b * n_tiles + i, 0))
    full = lambda a: pl.BlockSpec(a.shape, lambda b, i: (0,) * a.ndim)
    small = (conv_w, conv_b, w_gate, b_gate, lam, sc_w)
    out = jax.ShapeDtypeStruct((GROUP_LANES, n, LANE), BF16)
    return pl.pallas_call(
        _recurrent_kernel,
        grid=(batch, n_tiles),
        in_specs=[blk(LRU_X), blk(LRU_G), blk(SC_B), blk(SC_C), blk(SC_X)] + [full(a) for a in small],
        out_specs=[blk(0), blk(0)],
        out_shape=[out, out],
        scratch_shapes=[pltpu.VMEM((t + HALO, GROUP_W), F32),
                        pltpu.VMEM((t + HALO, GROUP_W), F32),
                        pltpu.VMEM((t, GROUP_W), F32),
                        pltpu.VMEM((t, GROUP_W), F32),
                        pltpu.VMEM((t, GROUP_W), F32),
                        pltpu.VMEM((8, GROUP_W), F32)],
        compiler_params=_params(("parallel", "arbitrary"), 32),
        name="recurrent",
    )(p, p, p, p, p, *small)


def _gate_weights(w_gate, b_gate):
    per_group = LANE // w_gate.shape[-1]
    zeros = jnp.zeros(w_gate.shape[-2:], w_gate.dtype)
    groups = []
    for c in range(GROUP_LANES):
        halves = []
        for gate in range(2):
            blocks = [w_gate[gate, c * per_group + j] for j in range(per_group)]
            halves.append(jnp.block([[blocks[a] if a == b else zeros for b in range(per_group)]
                                     for a in range(per_group)]))
        groups.append(jnp.concatenate(halves, axis=1))
    wg = jnp.stack(groups).astype(BF16)
    bg = jnp.concatenate([b_gate[0].reshape(GROUP_LANES, 1, LANE),
                          b_gate[1].reshape(GROUP_LANES, 1, LANE)], axis=-1).astype(F32)
    return wg, bg


def _outproj_kernel(sb_ref, df_ref, lru_ref, sc_ref, w_ref, x_ref, g1_ref, g2_ref,
                    xo_ref, h_ref, mixed):
    for a, ref in enumerate((sb_ref, df_ref, lru_ref, sc_ref)):
        for c in range(GROUP_LANES):
            col = (a * GROUP_LANES + c) * LANE
            mixed[:, col:col + LANE] = ref[c]
    y = jnp.dot(mixed[...], w_ref[...], preferred_element_type=F32)
    x_new = x_ref[...] + _rms(y, g1_ref[...])
    xo_ref[...] = x_new
    h_ref[...] = _rms(x_new, g2_ref[...]).astype(BF16)


def _outproj(y_sb, y_df, y_lru, y_sc, w, x2, g1, g2, tm=512):
    n, d = x2.shape
    blk = pl.BlockSpec((GROUP_LANES, tm, LANE), lambda i: (0, i, 0))
    row = pl.BlockSpec((tm, d), lambda i: (i, 0))
    vec = pl.BlockSpec((1, d), lambda i: (0, 0))
    return pl.pallas_call(
        _outproj_kernel,
        grid=(n // tm,),
        in_specs=[blk, blk, blk, blk, pl.BlockSpec(w.shape, lambda i: (0, 0)), row, vec, vec],
        out_specs=[row, row],
        out_shape=[jax.ShapeDtypeStruct((n, d), F32), jax.ShapeDtypeStruct((n, d), BF16)],
        scratch_shapes=[pltpu.VMEM((tm, w.shape[0]), BF16)],
        compiler_params=_params(("parallel",), 48),
        name="outproj",
    )(y_sb, y_df, y_lru, y_sc, w, x2, g1, g2)


def _ffn_up_kernel(h_ref, wg_ref, wu_ref, cg_ref, cu_ref, o_ref, ug_s, uu_s, carry_g, carry_u,
                   *, tiles_per_seq):
    i = pl.program_id(0)
    j = pl.program_id(1)
    tm, tf = o_ref.shape
    @pl.when(i == 0)
    def _():
        carry_g[j] = jnp.zeros(carry_g.shape[1:], F32)
        carry_u[j] = jnp.zeros(carry_u.shape[1:], F32)

    first = (i % tiles_per_seq) == 0
    ug_s[0:8, :] = jnp.where(first, 0.0, carry_g[j])
    uu_s[0:8, :] = jnp.where(first, 0.0, carry_u[j])

    n_blocks = tm // FFN_ROWS

    def project(c, k):
        cols = slice(c * MXU_COLS, (c + 1) * MXU_COLS)
        rb = k % n_blocks
        w_ref, dst_ref = ((wg_ref, ug_s), (wu_ref, uu_s))[k // n_blocks]
        dst_ref[8 + rb * FFN_ROWS:8 + (rb + 1) * FFN_ROWS, cols] = jnp.dot(
            h_ref[rb * FFN_ROWS:(rb + 1) * FFN_ROWS, :], w_ref[:, cols], preferred_element_type=F32)

    strips_per_dot = tm // (2 * n_blocks * FFN_STRIP)

    def activate(c, k):
        cols = slice(c * MXU_COLS, (c + 1) * MXU_COLS)
        src = slice(k * FFN_STRIP, (k + 1) * FFN_STRIP + 8)
        gate = _causal_conv(ug_s[src, cols], cg_ref[:, cols])
        up = _causal_conv(uu_s[src, cols], cu_ref[:, cols])
        o_ref[k * FFN_STRIP:(k + 1) * FFN_STRIP, cols] = (
            jax.nn.gelu(gate, approximate=True) * up).astype(BF16)

    n_chunks = tf // MXU_COLS
    for c in range(n_chunks + 1):
        for k in range(2 * n_blocks):
            if c < n_chunks:
                project(c, k)
            if c > 0:
                for s in range(k * strips_per_dot, (k + 1) * strips_per_dot):
                    activate(c - 1, s)
    carry_g[j] = ug_s[tm:tm + 8, :]
    carry_u[j] = uu_s[tm:tm + 8, :]


def _ffn_up(h2, w_up, conv_w, seq, tm=1024, tf=1024):
    n, d = h2.shape
    d_ff = w_up.shape[1] // 2
    n_f = d_ff // tf
    assert seq % tm == 0 and d_ff % tf == 0
    return pl.pallas_call(
        functools.partial(_ffn_up_kernel, tiles_per_seq=seq // tm),
        grid=(n // tm, n_f),
        in_specs=[
            pl.BlockSpec((tm, d), lambda i, j: (i, 0)),
            pl.BlockSpec((d, tf), lambda i, j: (0, j)),
            pl.BlockSpec((d, tf), lambda i, j: (0, n_f + j)),
            pl.BlockSpec((FFN_CONV, tf), lambda i, j: (0, j)),
            pl.BlockSpec((FFN_CONV, tf), lambda i, j: (0, n_f + j)),
        ],
        out_specs=pl.BlockSpec((tm, tf), lambda i, j: (i, j)),
        out_shape=jax.ShapeDtypeStruct((n, d_ff), BF16),
        scratch_shapes=[pltpu.VMEM((8 + tm, tf), F32), pltpu.VMEM((8 + tm, tf), F32),
                        pltpu.VMEM((n_f, 8, tf), F32), pltpu.VMEM((n_f, 8, tf), F32)],
        compiler_params=_params(("arbitrary", "arbitrary"), 48),
        name="ffn_up",
    )(h2, w_up, w_up, conv_w, conv_w)


def _ffn_down_kernel(a_ref, w_ref, x_ref, g_ref, o_ref):
    y = jnp.dot(a_ref[...], w_ref[...], preferred_element_type=F32)
    o_ref[...] = x_ref[...] + _rms(y, g_ref[...])


def _ffn_down(act, w_down, x2, gain, tm=512):
    n, d = x2.shape
    row = pl.BlockSpec((tm, d), lambda i: (i, 0))
    return pl.pallas_call(
        _ffn_down_kernel,
        grid=(n // tm,),
        in_specs=[
            pl.BlockSpec((tm, act.shape[1]), lambda i: (i, 0)),
            pl.BlockSpec(w_down.shape, lambda i: (0, 0), pipeline_mode=pl.Buffered(1)),
            row,
            pl.BlockSpec((1, d), lambda i: (0, 0)),
        ],
        out_specs=row,
        out_shape=jax.ShapeDtypeStruct((n, d), F32),
        compiler_params=_params(("parallel",), 48),
        name="ffn_down",
    )(act, w_down, x2, gain)


def _later_matrix():
    idx = np.arange(SB_GROUP * Q_BLOCK)
    return jnp.asarray(idx[:, None] > idx[None, :], BF16)


def kernel(x, norm_gains, w_in, w_out, rel_bias, diff_lambda, diff_subln_g, lru_conv_w, lru_conv_b,
           lru_w_gate, lru_b_gate, lru_lambda, sc_conv_w, ffn_w_up, ffn_conv_w, ffn_w_down):
    batch, seq, d = x.shape
    depth = w_in.shape[0]
    assert seq % ATT_TILE == 0 and w_in.shape[2] == N_IN_BLOCKS * GROUP_W
    x2 = x.reshape(batch * seq, d).astype(F32)
    tri = _later_matrix()
    bias = _diff_bias_table(rel_bias)
    row = lambda v: v.reshape(1, -1).astype(F32)
    for l in range(depth):
        lam_init = 0.8 - 0.6 * math.exp(-0.3 * l)
        p = _inproj(x2, row(norm_gains[l, 0]), _layer_bf16(w_in, l))
        y_sb = _sb_attention(p, tri, batch, seq)
        y_df = _diff_attention(p, bias, diff_lambda[l].astype(F32), row(diff_subln_g[l]),
                               lam_init, batch, seq)
        wg, bg = _gate_weights(lru_w_gate[l], lru_b_gate[l])
        y_lru, y_sc = _recurrent(p, lru_conv_w[l].astype(F32), row(lru_conv_b[l]), wg, bg,
                                 row(lru_lambda[l]), sc_conv_w[l].astype(F32), batch, seq)
        x2, h2 = _outproj(y_sb, y_df, y_lru, y_sc, _layer_bf16(w_out, l), x2,
                          row(norm_gains[l, 1]), row(norm_gains[l, 2]))
        act = _ffn_up(h2, _layer_bf16(ffn_w_up, l), ffn_conv_w[l].astype(F32), seq)
        x2 = _ffn_down(act, _layer_bf16(ffn_w_down, l), x2, row(norm_gains[l, 3]))
    return x2.reshape(batch, seq, d).astype(x.dtype)
```

```python
import functools
import math

import numpy as np
import jax
import jax.numpy as jnp
from jax import lax
from jax.experimental import pallas as pl
from jax.experimental.pallas import tpu as pltpu

F32 = jnp.float32
BF16 = jnp.bfloat16

LANE = 128
MXU_COLS = 256
FFN_STRIP = 256
FFN_ROWS = 512
HEAD_DIM = 64
GROUP_W = 512
GROUP_LANES = GROUP_W // LANE
N_IN_BLOCKS = 11
Q_BLOCK = 128
WINDOW = 1024
SPAN = WINDOW + Q_BLOCK
N_KEY_BLOCKS = SPAN // Q_BLOCK
SB_GROUP = 2
SB_NEAR_BLOCKS = 3
EXIT_LOG_WEIGHT = -110.0
ATT_TILE = 512
DIFF_HEADS_PER_STEP = 2
WIN_TILES = WINDOW // ATT_TILE + 1
CHUNK = 64
N_BUCKETS = 32
MAX_DISTANCE = 128
LRU_C = 8.0
LRU_CONV = 4
SC_CONV = 3
FFN_CONV = 3
CAST_BLOCK_BYTES = 4 << 20
HALO = 16
NORM_EPS = 1e-6
NEG_INF = -1e30

(SB_Q, SB_K, SB_V, DF_Q, DF_K, DF_V, LRU_X, LRU_G, SC_B, SC_C, SC_X) = range(N_IN_BLOCKS)


def _params(semantics, vmem_mib):
    return pltpu.CompilerParams(dimension_semantics=semantics, vmem_limit_bytes=vmem_mib << 20)


def _rms(v, gain):
    return v * lax.rsqrt(jnp.mean(v * v, axis=-1, keepdims=True) + NORM_EPS) * gain


def _nt_dot(a, b):
    return lax.dot_general(a, b, (((1,), (1,)), ((), ())), preferred_element_type=F32)


def _cast_kernel(w_ref, o_ref):
    o_ref[...] = w_ref[...].astype(BF16)


def _layer_bf16(w, layer):
    _, rows, cols = w.shape
    rb = 1 << ((CAST_BLOCK_BYTES // (4 * cols)).bit_length() - 1)
    assert rows % rb == 0 and rb % 16 == 0
    return pl.pallas_call(
        _cast_kernel,
        grid=(rows // rb,),
        in_specs=[pl.BlockSpec((None, rb, cols), lambda i: (layer, i, 0))],
        out_specs=pl.BlockSpec((rb, cols), lambda i: (i, 0)),
        out_shape=jax.ShapeDtypeStruct((rows, cols), BF16),
        compiler_params=_params(("parallel",), 32),
        name="weight_bf16",
    )(w)


def _inproj_kernel(x_ref, g_ref, w_ref, o_ref, h_ref):
    @pl.when(pl.program_id(1) == 0)
    def _():
        h_ref[...] = _rms(x_ref[...], g_ref[...]).astype(BF16)

    res = jnp.dot(h_ref[...], w_ref[...], preferred_element_type=F32)
    for c in range(o_ref.shape[0]):
        o_ref[c] = res[:, c * LANE:(c + 1) * LANE].astype(BF16)


def _inproj(x2, gain, w, tm=1024, tn=512):
    n, d = x2.shape
    cols = w.shape[1]
    n_rows = n // tm
    return pl.pallas_call(
        _inproj_kernel,
        grid=(n_rows, cols // tn),
        in_specs=[
            pl.BlockSpec((tm, d), lambda i, j: (jnp.minimum(i + jnp.minimum(j, 1), n_rows - 1), 0)),
            pl.BlockSpec((1, d), lambda i, j: (0, 0)),
            pl.BlockSpec((d, tn), lambda i, j: (0, j)),
        ],
        out_specs=pl.BlockSpec((tn // LANE, tm, LANE), lambda i, j: (j, i, 0)),
        out_shape=jax.ShapeDtypeStruct((cols // LANE, n, LANE), BF16),
        scratch_shapes=[pltpu.VMEM((tm, d), BF16)],
        compiler_params=_params(("parallel", "arbitrary"), 40),
        name="inproj",
    )(x2, gain, w)


def _fill_window(dst, tiles):
    for t, ref in enumerate(tiles):
        dst[t * ATT_TILE:(t + 1) * ATT_TILE, :] = ref[...]


def _mask_missing_tiles(i, set_cols):
    for missing in range(1, WIN_TILES):
        @pl.when(i == WIN_TILES - 1 - missing)
        def _(missing=missing):
            set_cols(missing * ATT_TILE)


def _half_select(lane, half):
    return (lane < HEAD_DIM) if half == 0 else (lane >= HEAD_DIM)


def _key_groups(lo, hi):
    groups = [(max(top - SB_GROUP, lo), top) for top in range(hi, lo, -SB_GROUP)]
    return groups


def _sb_weights(z_of, groups, later, tri_ref, newest_mask, oldest_mask):
    w_parts = []
    for lo, hi in groups:
        width = (hi - lo) * Q_BLOCK
        z = z_of(lo, hi)
        if hi == N_KEY_BLOCKS:
            z = z + newest_mask[:, newest_mask.shape[1] - width:]
        if lo == 0:
            oldest = z[:, :Q_BLOCK] + oldest_mask
            z = oldest if width == Q_BLOCK else jnp.concatenate([oldest, z[:, Q_BLOCK:]], axis=1)
        log_keep = jnp.minimum(-z, 0.0) - jnp.log(1.0 + jnp.exp(-jnp.abs(z)))
        within = jnp.dot(log_keep.astype(BF16), tri_ref[0:width, 0:width],
                         preferred_element_type=F32)
        after = within if later is None else within + jnp.concatenate([later] * (hi - lo), axis=1)
        w_parts.append(jnp.exp(z + log_keep + after).astype(BF16))
        total = jnp.broadcast_to(jnp.sum(log_keep, axis=-1, keepdims=True), (Q_BLOCK, LANE))
        later = total if later is None else later + total
    return jnp.concatenate(w_parts[::-1], axis=1), later


def _sb_near(znear, vwin, tri_ref, newest_mask, acc_s, later_s, keep_s, arg_s, within_s, newer_s, w_s):
    n_q = ATT_TILE // Q_BLOCK
    first_near = N_KEY_BLOCKS - SB_NEAR_BLOCKS
    near_groups = _key_groups(first_near, N_KEY_BLOCKS)
    cols_of = lambda lo, hi: slice((lo - first_near) * Q_BLOCK, (hi - first_near) * Q_BLOCK)
    pairs = [(half, r) for half in range(2) for r in range(n_q)]
    pair_rows = lambda c: slice(c * Q_BLOCK, (c + 1) * Q_BLOCK)

    worst = [None] * n_q
    for c, (half, r) in enumerate(pairs):
        later = None
        for g, (lo, hi) in enumerate(near_groups):
            z = znear[half, r * Q_BLOCK:(r + 1) * Q_BLOCK,
                      (r + lo - first_near) * Q_BLOCK:(r + hi - first_near) * Q_BLOCK]
            if hi == N_KEY_BLOCKS:
                z = z + newest_mask[:, newest_mask.shape[1] - (hi - lo) * Q_BLOCK:]
            log_keep = jnp.minimum(-z, 0.0) - jnp.log(1.0 + jnp.exp(-jnp.abs(z)))
            keep_s[pair_rows(c), cols_of(lo, hi)] = log_keep.astype(BF16)
            arg_s[pair_rows(c), cols_of(lo, hi)] = z + log_keep
            if g > 0:
                newer_s[g - 1, c] = later
            total = jnp.broadcast_to(jnp.sum(log_keep, axis=-1, keepdims=True), (Q_BLOCK, LANE))
            later = total if later is None else later + total
        later_s[c] = later
        worst[r] = later if worst[r] is None else jnp.maximum(worst[r], later)

    for lo, hi in near_groups:
        width = (hi - lo) * Q_BLOCK
        within_s[:, cols_of(lo, hi)] = jnp.dot(keep_s[:, cols_of(lo, hi)], tri_ref[0:width, 0:width],
                                               preferred_element_type=F32)

    for c, (half, r) in enumerate(pairs):
        for g, (lo, hi) in enumerate(near_groups):
            arg = arg_s[pair_rows(c), cols_of(lo, hi)] + within_s[pair_rows(c), cols_of(lo, hi)]
            if g > 0:
                arg = arg + jnp.concatenate([newer_s[g - 1, c]] * (hi - lo), axis=1)
            w_s[r, half * Q_BLOCK:(half + 1) * Q_BLOCK, cols_of(lo, hi)] = jnp.exp(arg).astype(BF16)

    for r in range(n_q):
        out = jnp.dot(w_s[r], vwin[(r + first_near) * Q_BLOCK:(r + N_KEY_BLOCKS) * Q_BLOCK, :],
                      preferred_element_type=F32)
        for half in range(2):
            acc_s[half, r * Q_BLOCK:(r + 1) * Q_BLOCK, :] = out[half * Q_BLOCK:(half + 1) * Q_BLOCK]

    return worst


def _sb_kernel(*refs):
    i = pl.program_id(1)
    operands, tri_ref, out_ref, scratch = refs[:9], refs[9], refs[10], refs[11:]
    kwin, knear, vwin, znear_a, znear_b = scratch[:5]

    def pair(g, carry):
        _sb_pair(i, *[ref.at[g] for ref in operands], tri_ref, out_ref.at[g],
                 kwin, knear, vwin, znear_a.at[g], znear_b.at[g], *scratch[5:])
        return carry

    lax.fori_loop(0, GROUP_LANES, pair, 0)


def _sb_pair(i, q_ref, qn_ref, k2_ref, k1_ref, k0_ref, kn_ref, v2_ref, v1_ref, v0_ref, tri_ref, o_ref,
             kwin, knear, vwin, znear_a, znear_b, zfar, acc_s, later_s, *near_scratch):
    lane = lax.broadcasted_iota(jnp.int32, (1, LANE), 1)
    row = lax.broadcasted_iota(jnp.int32, (Q_BLOCK, SB_GROUP * Q_BLOCK), 0)
    col = lax.broadcasted_iota(jnp.int32, (Q_BLOCK, SB_GROUP * Q_BLOCK), 1)
    newest_mask = jnp.where(col < row + (SB_GROUP - 1) * Q_BLOCK, 0.0, NEG_INF)
    oldest_mask = jnp.where(lax.broadcasted_iota(jnp.int32, (Q_BLOCK, Q_BLOCK), 1)
                            >= lax.broadcasted_iota(jnp.int32, (Q_BLOCK, Q_BLOCK), 0),
                            0.0, NEG_INF)
    n_q = ATT_TILE // Q_BLOCK
    first_near = N_KEY_BLOCKS - SB_NEAR_BLOCKS
    near_row0 = first_near * Q_BLOCK
    older = (WIN_TILES - 1) * ATT_TILE - near_row0

    def masked_queries(ref):
        q2 = ref[...] * (HEAD_DIM ** -0.5)
        return [jnp.where(_half_select(lane, half), q2, jnp.zeros_like(q2)) for half in range(2)]

    @pl.when(i == 0)
    def _():
        for half, qm in enumerate(masked_queries(q_ref)):
            znear_a[half, :, 0:older] = jnp.full((ATT_TILE, older), NEG_INF, F32)
            znear_a[half, :, older:] = _nt_dot(qm, k0_ref[...])

    def step(cur, nxt):
        knear[0:older, :] = k0_ref[ATT_TILE - older:, :]
        knear[older:, :] = kn_ref[...]
        for half, qm in enumerate(masked_queries(qn_ref)):
            nxt[half] = _nt_dot(qm, knear[...])

        _fill_window(vwin, (v2_ref, v1_ref, v0_ref))
        worst_of = _sb_near(cur, vwin, tri_ref, newest_mask, acc_s, later_s, *near_scratch)

        missing_rows = jnp.maximum(WIN_TILES - 1 - i, 0) * ATT_TILE
        worst = None
        for r in range(n_q):
            has_far_keys = (r + first_near) * Q_BLOCK > missing_rows
            worst_r = jnp.where(has_far_keys, worst_of[r], NEG_INF)
            worst = worst_r if worst is None else jnp.maximum(worst, worst_r)

        @pl.when(jnp.max(worst) >= EXIT_LOG_WEIGHT)
        def _():
            _fill_window(kwin, (k2_ref, k1_ref, k0_ref))
            for half, qm in enumerate(masked_queries(q_ref)):
                zfar[...] = _nt_dot(qm, kwin[0:(n_q - 1 + first_near) * Q_BLOCK, :])

                def _set(ncols):
                    zfar[:, 0:ncols] = jnp.full((ATT_TILE, ncols), NEG_INF, F32)
                _mask_missing_tiles(i, _set)

                for r in range(n_q):
                    rows = slice(r * Q_BLOCK, (r + 1) * Q_BLOCK)
                    z_of = lambda lo, hi, r=r, rows=rows: zfar[
                        rows, (r + lo) * Q_BLOCK:(r + hi) * Q_BLOCK]
                    w, _ = _sb_weights(z_of, _key_groups(0, first_near), later_s[half * n_q + r],
                                       tri_ref, newest_mask, oldest_mask)
                    acc_s[half, rows, :] += jnp.dot(
                        w, vwin[r * Q_BLOCK:(r + first_near) * Q_BLOCK, :],
                        preferred_element_type=F32)

        o_ref[...] = jnp.where(lane < HEAD_DIM, acc_s[0], acc_s[1]).astype(BF16)

    @pl.when(i % 2 == 0)
    def _():
        step(znear_a, znear_b)

    @pl.when(i % 2 == 1)
    def _():
        step(znear_b, znear_a)


def _sb_attention(p, tri, batch, seq):
    n = batch * seq
    n_tiles = seq // ATT_TILE
    n_q = ATT_TILE // Q_BLOCK
    near_row0 = (N_KEY_BLOCKS - SB_NEAR_BLOCKS) * Q_BLOCK
    far_rows = (n_q - 1 + N_KEY_BLOCKS - SB_NEAR_BLOCKS) * Q_BLOCK
    near_cols = SB_NEAR_BLOCKS * Q_BLOCK
    n_near_groups = len(_key_groups(N_KEY_BLOCKS - SB_NEAR_BLOCKS, N_KEY_BLOCKS))
    def tile(block, back):
        return pl.BlockSpec(
            (GROUP_LANES, ATT_TILE, LANE),
            lambda b, i: (block, b * n_tiles + jnp.clip(i - back, 0, n_tiles - 1), 0))

    near_rows = WIN_TILES * ATT_TILE - near_row0
    near_logits = pltpu.VMEM((GROUP_LANES, 2, ATT_TILE, near_rows), F32)
    window = list(range(WIN_TILES - 1, -1, -1))
    return pl.pallas_call(
        _sb_kernel,
        grid=(batch, n_tiles),
        in_specs=[tile(SB_Q, 0), tile(SB_Q, -1)]
        + [tile(SB_K, back) for back in window] + [tile(SB_K, -1)]
        + [tile(SB_V, back) for back in window]
        + [pl.BlockSpec(tri.shape, lambda b, i: (0, 0))],
        out_specs=tile(0, 0),
        out_shape=jax.ShapeDtypeStruct((GROUP_LANES, n, LANE), BF16),
        scratch_shapes=[pltpu.VMEM((WIN_TILES * ATT_TILE, LANE), BF16),
                        pltpu.VMEM((near_rows, LANE), BF16),
                        pltpu.VMEM((WIN_TILES * ATT_TILE, LANE), BF16),
                        near_logits, near_logits,
                        pltpu.VMEM((ATT_TILE, far_rows), F32),
                        pltpu.VMEM((2, ATT_TILE, LANE), F32),
                        pltpu.VMEM((2 * n_q, Q_BLOCK, LANE), F32),
                        pltpu.VMEM((2 * ATT_TILE, near_cols), BF16),
                        pltpu.VMEM((2 * ATT_TILE, near_cols), F32),
                        pltpu.VMEM((2 * ATT_TILE, near_cols), F32),
                        pltpu.VMEM((n_near_groups - 1, 2 * n_q, Q_BLOCK, LANE), F32),
                        pltpu.VMEM((n_q, 2 * Q_BLOCK, near_cols), BF16)],
        compiler_params=_params(("parallel", "arbitrary"), 56),
        name="sb_attention",
    )(*([p] * 9), tri)


def _bias_kernel(rb_ref, bucket_ref, band_ref, o_ref):
    bucket = bucket_ref[...]
    band = band_ref[...] > 0
    for h in range(o_ref.shape[0]):
        acc = jnp.zeros(bucket.shape, F32)
        for b in range(N_BUCKETS):
            acc = jnp.where(bucket == b, rb_ref[b, h], acc)
        o_ref[h] = jnp.where(band, acc, NEG_INF)


def _t5_bucket(rel):
    nb = N_BUCKETS // 2
    max_exact = nb // 2
    ret = jnp.where(rel > 0, nb, 0)
    n = jnp.abs(rel)
    large = max_exact + (jnp.log(jnp.maximum(n, 1).astype(jnp.float32) / max_exact)
                         / math.log(MAX_DISTANCE / max_exact) * (nb - max_exact)).astype(jnp.int32)
    large = jnp.minimum(large, nb - 1)
    return ret + jnp.where(n < max_exact, n, large)


def _diff_bias_table(rel_bias):
    heads = rel_bias.shape[1]
    a_key = jnp.arange(SPAN) - WINDOW
    a_qry = jnp.arange(Q_BLOCK)
    bucket = _t5_bucket(a_key[None, :] - a_qry[:, None]).astype(jnp.int32)
    kc = (a_key // CHUNK)[None, :]
    qc = (a_qry // CHUNK)[:, None]
    band = ((kc <= qc) & (kc >= qc - WINDOW // CHUNK)).astype(jnp.int32)
    return pl.pallas_call(
        _bias_kernel,
        in_specs=[pl.BlockSpec(memory_space=pltpu.SMEM),
                  pl.BlockSpec((Q_BLOCK, SPAN), lambda: (0, 0)),
                  pl.BlockSpec((Q_BLOCK, SPAN), lambda: (0, 0))],
        out_specs=pl.BlockSpec((heads, Q_BLOCK, SPAN), lambda: (0, 0, 0)),
        out_shape=jax.ShapeDtypeStruct((heads, Q_BLOCK, SPAN), F32),
        name="diff_bias_table",
    )(rel_bias.astype(F32), bucket, band)


def _diff_kernel(*refs, lam_init):
    operands, (bias_ref, lam_ref, g_ref, out_ref) = refs[:9], refs[9:13]
    kwin, vwin, logits_a, logits_b, lam_s = refs[13:]

    def head(h, carry):
        _diff_head(*[ref.at[h] for ref in operands], bias_ref.at[h], lam_ref, g_ref, out_ref.at[h],
                   kwin, vwin, logits_a.at[h], logits_b.at[h], lam_s, lam_init=lam_init)
        return carry

    lax.fori_loop(0, DIFF_HEADS_PER_STEP, head, 0)


def _diff_head(q_ref, qn_ref, k2_ref, k1_ref, k0_ref, kn_ref, v2_ref, v1_ref, v0_ref,
               bias_ref, lam_ref, g_ref, o_ref, kwin, vwin, logits_a, logits_b, lam_s, *, lam_init):
    i = pl.program_id(2)
    lane = lax.broadcasted_iota(jnp.int32, (1, LANE), 1)
    gain = g_ref[...] * (1.0 - lam_init)

    @pl.when(i == 0)
    def _():
        lv = lam_ref[...]
        lam0 = (jnp.exp(jnp.sum(lv[0:1] * lv[1:2], axis=-1, keepdims=True))
                - jnp.exp(jnp.sum(lv[2:3] * lv[3:4], axis=-1, keepdims=True)) + lam_init)
        lam_s[...] = jnp.broadcast_to(lam0, lam_s.shape)

    lam = lam_s[0:1, 0:1]

    def masked_queries(ref):
        q2 = ref[...] * (HEAD_DIM ** -0.5)
        return [jnp.where(_half_select(lane, m), q2, jnp.zeros_like(q2)) for m in range(2)]

    @pl.when(i == 0)
    def _():
        _fill_window(kwin, (k2_ref, k1_ref, k0_ref))
        ncols = (WIN_TILES - 1) * ATT_TILE
        for m, qm in enumerate(masked_queries(q_ref)):
            logits_a[m, :, 0:ncols] = jnp.full((ATT_TILE, ncols), NEG_INF, F32)
            logits_a[m, :, ncols:] = _nt_dot(qm, kwin[ncols:, :])

    def step(cur, nxt):
        _fill_window(kwin, (k1_ref, k0_ref, kn_ref))
        for m, qm in enumerate(masked_queries(qn_ref)):
            res = _nt_dot(qm, kwin[...])
            nxt[m, :, 0:ATT_TILE] = jnp.where(i == 0, NEG_INF, res[:, 0:ATT_TILE])
            nxt[m, :, ATT_TILE:] = res[:, ATT_TILE:]

        _fill_window(vwin, (v2_ref, v1_ref, v0_ref))
        bias = bias_ref[...]
        for r in range(ATT_TILE // Q_BLOCK):
            rows = slice(r * Q_BLOCK, (r + 1) * Q_BLOCK)
            probs = []
            for m in range(2):
                logits = cur[m, rows, r * Q_BLOCK:r * Q_BLOCK + SPAN] + bias
                e = jnp.exp(logits - jnp.max(logits, axis=-1, keepdims=True))
                probs.append((e, jnp.sum(e, axis=-1, keepdims=True)))
            attn = probs[0][0] * (1.0 / probs[0][1]) - probs[1][0] * (lam / probs[1][1])
            out = jnp.dot(attn.astype(BF16), vwin[r * Q_BLOCK:r * Q_BLOCK + SPAN, :],
                          preferred_element_type=F32)
            o_ref[rows, :] = _rms(out, gain).astype(BF16)

    @pl.when(i % 2 == 0)
    def _():
        step(logits_a, logits_b)

    @pl.when(i % 2 == 1)
    def _():
        step(logits_b, logits_a)


def _diff_attention(p, bias, lam_params, subln_gain, lam_init, batch, seq):
    n = batch * seq
    n_tiles = seq // ATT_TILE
    heads = bias.shape[0]
    per = DIFF_HEADS_PER_STEP
    assert heads % per == 0 and GROUP_LANES % per == 0

    def tile(block, back):
        return pl.BlockSpec(
            (per, ATT_TILE, LANE),
            lambda b, g, i: (block * (GROUP_LANES // per) + g,
                             b * n_tiles + jnp.clip(i - back, 0, n_tiles - 1), 0))

    window = list(range(WIN_TILES - 1, -1, -1))
    logits = pltpu.VMEM((per, 2, ATT_TILE, WIN_TILES * ATT_TILE), F32)
    return pl.pallas_call(
        functools.partial(_diff_kernel, lam_init=lam_init),
        grid=(batch, heads // per, n_tiles),
        in_specs=[tile(DF_Q, 0), tile(DF_Q, -1)]
        + [tile(DF_K, back) for back in window] + [tile(DF_K, -1)]
        + [tile(DF_V, back) for back in window]
        + [pl.BlockSpec((per, Q_BLOCK, SPAN), lambda b, g, i: (g, 0, 0)),
           pl.BlockSpec(lam_params.shape, lambda b, g, i: (0, 0)),
           pl.BlockSpec((1, LANE), lambda b, g, i: (0, 0))],
        out_specs=tile(0, 0),
        out_shape=jax.ShapeDtypeStruct((heads, n, LANE), BF16),
        scratch_shapes=[pltpu.VMEM((WIN_TILES * ATT_TILE, LANE), BF16),
                        pltpu.VMEM((WIN_TILES * ATT_TILE, LANE), BF16),
                        logits, logits, pltpu.VMEM((8, LANE), F32)],
        compiler_params=_params(("parallel", "parallel", "arbitrary"), 48),
        name="diff_attention",
    )(*([p] * 9), bias, lam_params, subln_gain)


def _softplus(v):
    return jnp.maximum(v, 0.0) + jnp.log1p(jnp.exp(-jnp.abs(v)))


def _expm1(y):
    u = jnp.exp(y)
    is_one = u == 1.0
    near_zero = jnp.where(is_one, y, (u - 1.0) * y / jnp.where(is_one, 1.0, jnp.log(u)))
    return jnp.where(y > -0.5, near_zero, u - 1.0)


def _causal_conv(padded, taps):
    n_taps = taps.shape[0]
    rows, width = padded.shape
    groups = padded.reshape(rows // 8, 8, width)
    sub = lax.broadcasted_iota(jnp.int32, (1, 8, width), 1)
    out = groups[1:] * taps[n_taps - 1:n_taps]
    for k in range(n_taps - 1):
        back = n_taps - 1 - k
        turned = pltpu.roll(groups, back, 1)
        out = out + jnp.where(sub >= back, turned[1:], turned[:-1]) * taps[k:k + 1]
    return out.reshape(rows - 8, width)


def _recurrent_kernel(x_ref, gate_ref, scb_ref, scc_ref, scx_ref,
                      cw_ref, cb_ref, wg_ref, bg_ref, lam_ref, scw_ref,
                      ylru_ref, ysc_ref,
                      xext, sext, a_s, b_s, h_s, carry):
    t = x_ref.shape[1]

    @pl.when(pl.program_id(1) == 0)
    def _():
        xext[0:HALO, :] = jnp.zeros((HALO, GROUP_W), F32)
        sext[0:HALO, :] = jnp.zeros((HALO, GROUP_W), F32)
        carry[...] = jnp.zeros(carry.shape, F32)

    for c in range(GROUP_LANES):
        lanes = slice(c * LANE, (c + 1) * LANE)
        xext[HALO:, lanes] = x_ref[c].astype(F32)
        sext[HALO:, lanes] = scc_ref[c].astype(F32) * scx_ref[c].astype(F32)

    for c in range(GROUP_LANES):
        lanes = slice(c * LANE, (c + 1) * LANE)
        xc = cb_ref[:, lanes] + _causal_conv(xext[HALO - 8:, lanes], cw_ref[:, lanes])
        gates = jax.nn.sigmoid(
            jnp.dot(xc.astype(BF16), wg_ref[c], preferred_element_type=F32) + bg_ref[c])
        r_gate = gates[:, :LANE]
        i_gate = gates[:, LANE:]
        log_a = (-LRU_C) * r_gate * _softplus(-lam_ref[:, lanes])
        a_s[:, lanes] = jnp.exp(log_a)
        b_s[:, lanes] = jnp.sqrt(-_expm1(2.0 * log_a)) * (i_gate * xc)

        sc = _causal_conv(sext[HALO - 8:, lanes], scw_ref[:, lanes])
        ysc_ref[c] = (scb_ref[c].astype(F32) * sc).astype(BF16)

    xext[0:HALO, :] = xext[t:t + HALO, :]
    sext[0:HALO, :] = sext[t:t + HALO, :]

    sub = lax.broadcasted_iota(jnp.int32, (8, GROUP_W), 0)

    def step(j, h_prev):
        rows = pl.ds(pl.multiple_of(j * 8, 8), 8)
        a = a_s[rows, :]
        b = b_s[rows, :]
        for d in (1, 2, 4):
            keep = sub >= d
            b = a * jnp.where(keep, pltpu.roll(b, d, 0), 0.0) + b
            a = a * jnp.where(keep, pltpu.roll(a, d, 0), 1.0)
        h = a * h_prev + b
        h_s[rows, :] = h
        return jnp.broadcast_to(h[7:8, :], (8, GROUP_W))

    carry[...] = lax.fori_loop(0, t // 8, step, carry[...], unroll=4)

    for c in range(GROUP_LANES):
        lanes = slice(c * LANE, (c + 1) * LANE)
        ylru_ref[c] = (h_s[:, lanes]
                       * jax.nn.gelu(gate_ref[c].astype(F32), approximate=True)).astype(BF16)


def _recurrent(p, conv_w, conv_b, w_gate, b_gate, lam, sc_w, batch, seq, t=512):
    n = batch * seq
    n_tiles = seq // t
    blk = lambda which: pl.BlockSpec((GROUP_LANES, t, LANE),
                                     lambda b, i: (which, b * n_tiles + i, 0))
    full = lambda a: pl.BlockSpec(a.shape, lambda b, i: (0,) * a.ndim)
    small = (conv_w, conv_b, w_gate, b_gate, lam, sc_w)
    out = jax.ShapeDtypeStruct((GROUP_LANES, n, LANE), BF16)
    return pl.pallas_call(
        _recurrent_kernel,
        grid=(batch, n_tiles),
        in_specs=[blk(LRU_X), blk(LRU_G), blk(SC_B), blk(SC_C), blk(SC_X)] + [full(a) for a in small],
        out_specs=[blk(0), blk(0)],
        out_shape=[out, out],
        scratch_shapes=[pltpu.VMEM((t + HALO, GROUP_W), F32),
                        pltpu.VMEM((t + HALO, GROUP_W), F32),
                        pltpu.VMEM((t, GROUP_W), F32),
                        pltpu.VMEM((t, GROUP_W), F32),
                        pltpu.VMEM((t, GROUP_W), F32),
                        pltpu.VMEM((8, GROUP_W), F32)],
        compiler_params=_params(("parallel", "arbitrary"), 32),
        name="recurrent",
    )(p, p, p, p, p, *small)


def _gate_weights(w_gate, b_gate):
    per_group = LANE // w_gate.shape[-1]
    zeros = jnp.zeros(w_gate.shape[-2:], w_gate.dtype)
    groups = []
    for c in range(GROUP_LANES):
        halves = []
        for gate in range(2):
            blocks = [w_gate[gate, c * per_group + j] for j in range(per_group)]
            halves.append(jnp.block([[blocks[a] if a == b else zeros for b in range(per_group)]
                                     for a in range(per_group)]))
        groups.append(jnp.concatenate(halves, axis=1))
    wg = jnp.stack(groups).astype(BF16)
    bg = jnp.concatenate([b_gate[0].reshape(GROUP_LANES, 1, LANE),
                          b_gate[1].reshape(GROUP_LANES, 1, LANE)], axis=-1).astype(F32)
    return wg, bg


def _outproj_kernel(sb_ref, df_ref, lru_ref, sc_ref, w_ref, x_ref, g1_ref, g2_ref,
                    xo_ref, h_ref, mixed):
    for a, ref in enumerate((sb_ref, df_ref, lru_ref, sc_ref)):
        for c in range(GROUP_LANES):
            col = (a * GROUP_LANES + c) * LANE
            mixed[:, col:col + LANE] = ref[c]
    y = jnp.dot(mixed[...], w_ref[...], preferred_element_type=F32)
    x_new = x_ref[...] + _rms(y, g1_ref[...])
    xo_ref[...] = x_new
    h_ref[...] = _rms(x_new, g2_ref[...]).astype(BF16)


def _outproj(y_sb, y_df, y_lru, y_sc, w, x2, g1, g2, tm=512):
    n, d = x2.shape
    blk = pl.BlockSpec((GROUP_LANES, tm, LANE), lambda i: (0, i, 0))
    row = pl.BlockSpec((tm, d), lambda i: (i, 0))
    vec = pl.BlockSpec((1, d), lambda i: (0, 0))
    return pl.pallas_call(
        _outproj_kernel,
        grid=(n // tm,),
        in_specs=[blk, blk, blk, blk, pl.BlockSpec(w.shape, lambda i: (0, 0)), row, vec, vec],
        out_specs=[row, row],
        out_shape=[jax.ShapeDtypeStruct((n, d), F32), jax.ShapeDtypeStruct((n, d), BF16)],
        scratch_shapes=[pltpu.VMEM((tm, w.shape[0]), BF16)],
        compiler_params=_params(("parallel",), 48),
        name="outproj",
    )(y_sb, y_df, y_lru, y_sc, w, x2, g1, g2)


def _ffn_up_kernel(h_ref, wg_ref, wu_ref, cg_ref, cu_ref, o_ref, ug_s, uu_s, carry_g, carry_u,
                   *, tiles_per_seq):
    i = pl.program_id(0)
    j = pl.program_id(1)
    tm, tf = o_ref.shape
    @pl.when(i == 0)
    def _():
        carry_g[j] = jnp.zeros(carry_g.shape[1:], F32)
        carry_u[j] = jnp.zeros(carry_u.shape[1:], F32)

    first = (i % tiles_per_seq) == 0
    ug_s[0:8, :] = jnp.where(first, 0.0, carry_g[j])
    uu_s[0:8, :] = jnp.where(first, 0.0, carry_u[j])

    n_blocks = tm // FFN_ROWS

    def project(c, k):
        cols = slice(c * MXU_COLS, (c + 1) * MXU_COLS)
        rb = k % n_blocks
        w_ref, dst_ref = ((wg_ref, ug_s), (wu_ref, uu_s))[k // n_blocks]
        dst_ref[8 + rb * FFN_ROWS:8 + (rb + 1) * FFN_ROWS, cols] = jnp.dot(
            h_ref[rb * FFN_ROWS:(rb + 1) * FFN_ROWS, :], w_ref[:, cols], preferred_element_type=F32)

    strips_per_dot = tm // (2 * n_blocks * FFN_STRIP)

    def activate(c, k):
        cols = slice(c * MXU_COLS, (c + 1) * MXU_COLS)
        src = slice(k * FFN_STRIP, (k + 1) * FFN_STRIP + 8)
        gate = _causal_conv(ug_s[src, cols], cg_ref[:, cols])
        up = _causal_conv(uu_s[src, cols], cu_ref[:, cols])
        o_ref[k * FFN_STRIP:(k + 1) * FFN_STRIP, cols] = (
            jax.nn.gelu(gate, approximate=True) * up).astype(BF16)

    n_chunks = tf // MXU_COLS
    for c in range(n_chunks + 1):
        for k in range(2 * n_blocks):
            if c < n_chunks:
                project(c, k)
            if c > 0:
                for s in range(k * strips_per_dot, (k + 1) * strips_per_dot):
                    activate(c - 1, s)
    carry_g[j] = ug_s[tm:tm + 8, :]
    carry_u[j] = uu_s[tm:tm + 8, :]


def _ffn_up(h2, w_up, conv_w, seq, tm=1024, tf=1024):
    n, d = h2.shape
    d_ff = w_up.shape[1] // 2
    n_f = d_ff // tf
    assert seq % tm == 0 and d_ff % tf == 0
    return pl.pallas_call(
        functools.partial(_ffn_up_kernel, tiles_per_seq=seq // tm),
        grid=(n // tm, n_f),
        in_specs=[
            pl.BlockSpec((tm, d), lambda i, j: (i, 0)),
            pl.BlockSpec((d, tf), lambda i, j: (0, j)),
            pl.BlockSpec((d, tf), lambda i, j: (0, n_f + j)),
            pl.BlockSpec((FFN_CONV, tf), lambda i, j: (0, j)),
            pl.BlockSpec((FFN_CONV, tf), lambda i, j: (0, n_f + j)),
        ],
        out_specs=pl.BlockSpec((tm, tf), lambda i, j: (i, j)),
        out_shape=jax.ShapeDtypeStruct((n, d_ff), BF16),
        scratch_shapes=[pltpu.VMEM((8 + tm, tf), F32), pltpu.VMEM((8 + tm, tf), F32),
                        pltpu.VMEM((n_f, 8, tf), F32), pltpu.VMEM((n_f, 8, tf), F32)],
        compiler_params=_params(("arbitrary", "arbitrary"), 48),
        name="ffn_up",
    )(h2, w_up, w_up, conv_w, conv_w)


def _ffn_down_kernel(a_ref, w_ref, x_ref, g_ref, o_ref):
    y = jnp.dot(a_ref[...], w_ref[...], preferred_element_type=F32)
    o_ref[...] = x_ref[...] + _rms(y, g_ref[...])


def _ffn_down(act, w_down, x2, gain, tm=512):
    n, d = x2.shape
    row = pl.BlockSpec((tm, d), lambda i: (i, 0))
    return pl.pallas_call(
        _ffn_down_kernel,
        grid=(n // tm,),
        in_specs=[
            pl.BlockSpec((tm, act.shape[1]), lambda i: (i, 0)),
            pl.BlockSpec(w_down.shape, lambda i: (0, 0), pipeline_mode=pl.Buffered(1)),
            row,
            pl.BlockSpec((1, d), lambda i: (0, 0)),
        ],
        out_specs=row,
        out_shape=jax.ShapeDtypeStruct((n, d), F32),
        compiler_params=_params(("parallel",), 48),
        name="ffn_down",
    )(act, w_down, x2, gain)


def _later_matrix():
    idx = np.arange(SB_GROUP * Q_BLOCK)
    return jnp.asarray(idx[:, None] > idx[None, :], BF16)


def kernel(x, norm_gains, w_in, w_out, rel_bias, diff_lambda, diff_subln_g, lru_conv_w, lru_conv_b,
           lru_w_gate, lru_b_gate, lru_lambda, sc_conv_w, ffn_w_up, ffn_conv_w, ffn_w_down):
    batch, seq, d = x.shape
    depth = w_in.shape[0]
    assert seq % ATT_TILE == 0 and w_in.shape[2] == N_IN_BLOCKS * GROUP_W
    x2 = x.reshape(batch * seq, d).astype(F32)
    tri = _later_matrix()
    bias = _diff_bias_table(rel_bias)
    row = lambda v: v.reshape(1, -1).astype(F32)
    for l in range(depth):
        lam_init = 0.8 - 0.6 * math.exp(-0.3 * l)
        p = _inproj(x2, row(norm_gains[l, 0]), _layer_bf16(w_in, l))
        y_sb = _sb_attention(p, tri, batch, seq)
        y_df = _diff_attention(p, bias, diff_lambda[l].astype(F32), row(diff_subln_g[l]),
                               lam_init, batch, seq)
        wg, bg = _gate_weights(lru_w_gate[l], lru_b_gate[l])
        y_lru, y_sc = _recurrent(p, lru_conv_w[l].astype(F32), row(lru_conv_b[l]), wg, bg,
                                 row(lru_lambda[l]), sc_conv_w[l].astype(F32), batch, seq)
        x2, h2 = _outproj(y_sb, y_df, y_lru, y_sc, _layer_bf16(w_out, l), x2,
                          row(norm_gains[l, 1]), row(norm_gains[l, 2]))
        act = _ffn_up(h2, _layer_bf16(ffn_w_up, l), ffn_conv_w[l].astype(F32), seq)
        x2 = _ffn_down(act, _layer_bf16(ffn_w_down, l), x2, row(norm_gains[l, 3]))
    return x2.reshape(batch, seq, d).astype(x.dtype)
```

```python
import functools
import math

import numpy as np
import jax
import jax.numpy as jnp
from jax import lax
from jax.experimental import pallas as pl
from jax.experimental.pallas import tpu as pltpu

F32 = jnp.float32
BF16 = jnp.bfloat16

LANE = 128
MXU_COLS = 256
FFN_STRIP = 256
FFN_ROWS = 512
HEAD_DIM = 64
GROUP_W = 512
GROUP_LANES = GROUP_W // LANE
N_IN_BLOCKS = 11
Q_BLOCK = 128
WINDOW = 1024
SPAN = WINDOW + Q_BLOCK
N_KEY_BLOCKS = SPAN // Q_BLOCK
SB_GROUP = 2
SB_NEAR_BLOCKS = 3
EXIT_LOG_WEIGHT = -110.0
ATT_TILE = 512
DIFF_HEADS_PER_STEP = 1
WIN_TILES = WINDOW // ATT_TILE + 1
CHUNK = 64
N_BUCKETS = 32
MAX_DISTANCE = 128
LRU_C = 8.0
LRU_CONV = 4
SC_CONV = 3
FFN_CONV = 3
CAST_BLOCK_BYTES = 4 << 20
HALO = 16
NORM_EPS = 1e-6
NEG_INF = -1e30

(SB_Q, SB_K, SB_V, DF_Q, DF_K, DF_V, LRU_X, LRU_G, SC_B, SC_C, SC_X) = range(N_IN_BLOCKS)


def _params(semantics, vmem_mib):
    return pltpu.CompilerParams(dimension_semantics=semantics, vmem_limit_bytes=vmem_mib << 20)


def _rms(v, gain):
    return v * lax.rsqrt(jnp.mean(v * v, axis=-1, keepdims=True) + NORM_EPS) * gain


def _nt_dot(a, b):
    return lax.dot_general(a, b, (((1,), (1,)), ((), ())), preferred_element_type=F32)


def _cast_kernel(w_ref, o_ref):
    o_ref[...] = w_ref[...].astype(BF16)


def _layer_bf16(w, layer):
    _, rows, cols = w.shape
    rb = 1 << ((CAST_BLOCK_BYTES // (4 * cols)).bit_length() - 1)
    assert rows % rb == 0 and rb % 16 == 0
    return pl.pallas_call(
        _cast_kernel,
        grid=(rows // rb,),
        in_specs=[pl.BlockSpec((None, rb, cols), lambda i: (layer, i, 0))],
        out_specs=pl.BlockSpec((rb, cols), lambda i: (i, 0)),
        out_shape=jax.ShapeDtypeStruct((rows, cols), BF16),
        compiler_params=_params(("parallel",), 32),
        name="weight_bf16",
    )(w)


def _inproj_kernel(x_ref, g_ref, w_ref, o_ref, h_ref):
    @pl.when(pl.program_id(1) == 0)
    def _():
        h_ref[...] = _rms(x_ref[...], g_ref[...]).astype(BF16)

    res = jnp.dot(h_ref[...], w_ref[...], preferred_element_type=F32)
    for c in range(o_ref.shape[0]):
        o_ref[c] = res[:, c * LANE:(c + 1) * LANE].astype(BF16)


def _inproj(x2, gain, w, tm=1024, tn=512):
    n, d = x2.shape
    cols = w.shape[1]
    n_rows = n // tm
    return pl.pallas_call(
        _inproj_kernel,
        grid=(n_rows, cols // tn),
        in_specs=[
            pl.BlockSpec((tm, d), lambda i, j: (jnp.minimum(i + jnp.minimum(j, 1), n_rows - 1), 0)),
            pl.BlockSpec((1, d), lambda i, j: (0, 0)),
            pl.BlockSpec((d, tn), lambda i, j: (0, j)),
        ],
        out_specs=pl.BlockSpec((tn // LANE, tm, LANE), lambda i, j: (j, i, 0)),
        out_shape=jax.ShapeDtypeStruct((cols // LANE, n, LANE), BF16),
        scratch_shapes=[pltpu.VMEM((tm, d), BF16)],
        compiler_params=_params(("parallel", "arbitrary"), 40),
        name="inproj",
    )(x2, gain, w)


def _fill_window(dst, tiles):
    for t, ref in enumerate(tiles):
        dst[t * ATT_TILE:(t + 1) * ATT_TILE, :] = ref[...]


def _mask_missing_tiles(i, set_cols):
    for missing in range(1, WIN_TILES):
        @pl.when(i == WIN_TILES - 1 - missing)
        def _(missing=missing):
            set_cols(missing * ATT_TILE)


def _half_select(lane, half):
    return (lane < HEAD_DIM) if half == 0 else (lane >= HEAD_DIM)


def _key_groups(lo, hi):
    groups = [(max(top - SB_GROUP, lo), top) for top in range(hi, lo, -SB_GROUP)]
    return groups


def _sb_weights(z_of, groups, later, tri_ref, newest_mask, oldest_mask):
    w_parts = []
    for lo, hi in groups:
        width = (hi - lo) * Q_BLOCK
        z = z_of(lo, hi)
        if hi == N_KEY_BLOCKS:
            z = z + newest_mask[:, newest_mask.shape[1] - width:]
        if lo == 0:
            oldest = z[:, :Q_BLOCK] + oldest_mask
            z = oldest if width == Q_BLOCK else jnp.concatenate([oldest, z[:, Q_BLOCK:]], axis=1)
        log_keep = jnp.minimum(-z, 0.0) - jnp.log(1.0 + jnp.exp(-jnp.abs(z)))
        within = jnp.dot(log_keep.astype(BF16), tri_ref[0:width, 0:width],
                         preferred_element_type=F32)
        after = within if later is None else within + jnp.concatenate([later] * (hi - lo), axis=1)
        w_parts.append(jnp.exp(z + log_keep + after).astype(BF16))
        total = jnp.broadcast_to(jnp.sum(log_keep, axis=-1, keepdims=True), (Q_BLOCK, LANE))
        later = total if later is None else later + total
    return jnp.concatenate(w_parts[::-1], axis=1), later


def _sb_near(znear, vwin, tri_ref, newest_mask, acc_s, later_s, keep_s, arg_s, within_s, newer_s, w_s):
    n_q = ATT_TILE // Q_BLOCK
    first_near = N_KEY_BLOCKS - SB_NEAR_BLOCKS
    near_groups = _key_groups(first_near, N_KEY_BLOCKS)
    cols_of = lambda lo, hi: slice((lo - first_near) * Q_BLOCK, (hi - first_near) * Q_BLOCK)
    pairs = [(half, r) for half in range(2) for r in range(n_q)]
    pair_rows = lambda c: slice(c * Q_BLOCK, (c + 1) * Q_BLOCK)

    worst = [None] * n_q
    for c, (half, r) in enumerate(pairs):
        later = None
        for g, (lo, hi) in enumerate(near_groups):
            z = znear[half, r * Q_BLOCK:(r + 1) * Q_BLOCK,
                      (r + lo - first_near) * Q_BLOCK:(r + hi - first_near) * Q_BLOCK]
            if hi == N_KEY_BLOCKS:
                z = z + newest_mask[:, newest_mask.shape[1] - (hi - lo) * Q_BLOCK:]
            log_keep = jnp.minimum(-z, 0.0) - jnp.log(1.0 + jnp.exp(-jnp.abs(z)))
            keep_s[pair_rows(c), cols_of(lo, hi)] = log_keep.astype(BF16)
            arg_s[pair_rows(c), cols_of(lo, hi)] = z + log_keep
            if g > 0:
                newer_s[g - 1, c] = later
            total = jnp.broadcast_to(jnp.sum(log_keep, axis=-1, keepdims=True), (Q_BLOCK, LANE))
            later = total if later is None else later + total
        later_s[c] = later
        worst[r] = later if worst[r] is None else jnp.maximum(worst[r], later)

    for lo, hi in near_groups:
        width = (hi - lo) * Q_BLOCK
        within_s[:, cols_of(lo, hi)] = jnp.dot(keep_s[:, cols_of(lo, hi)], tri_ref[0:width, 0:width],
                                               preferred_element_type=F32)

    for c, (half, r) in enumerate(pairs):
        for g, (lo, hi) in enumerate(near_groups):
            arg = arg_s[pair_rows(c), cols_of(lo, hi)] + within_s[pair_rows(c), cols_of(lo, hi)]
            if g > 0:
                arg = arg + jnp.concatenate([newer_s[g - 1, c]] * (hi - lo), axis=1)
            w_s[r, half * Q_BLOCK:(half + 1) * Q_BLOCK, cols_of(lo, hi)] = jnp.exp(arg).astype(BF16)

    for r in range(n_q):
        out = jnp.dot(w_s[r], vwin[(r + first_near) * Q_BLOCK:(r + N_KEY_BLOCKS) * Q_BLOCK, :],
                      preferred_element_type=F32)
        for half in range(2):
            acc_s[half, r * Q_BLOCK:(r + 1) * Q_BLOCK, :] = out[half * Q_BLOCK:(half + 1) * Q_BLOCK]

    return worst


def _sb_kernel(*refs):
    i = pl.program_id(1)
    operands, tri_ref, out_ref, scratch = refs[:9], refs[9], refs[10], refs[11:]
    kwin, knear, vwin, znear_a, znear_b = scratch[:5]

    def pair(g, carry):
        _sb_pair(i, *[ref.at[g] for ref in operands], tri_ref, out_ref.at[g],
                 kwin, knear, vwin, znear_a.at[g], znear_b.at[g], *scratch[5:])
        return carry

    lax.fori_loop(0, GROUP_LANES, pair, 0)


def _sb_pair(i, q_ref, qn_ref, k2_ref, k1_ref, k0_ref, kn_ref, v2_ref, v1_ref, v0_ref, tri_ref, o_ref,
             kwin, knear, vwin, znear_a, znear_b, zfar, acc_s, later_s, *near_scratch):
    lane = lax.broadcasted_iota(jnp.int32, (1, LANE), 1)
    row = lax.broadcasted_iota(jnp.int32, (Q_BLOCK, SB_GROUP * Q_BLOCK), 0)
    col = lax.broadcasted_iota(jnp.int32, (Q_BLOCK, SB_GROUP * Q_BLOCK), 1)
    newest_mask = jnp.where(col < row + (SB_GROUP - 1) * Q_BLOCK, 0.0, NEG_INF)
    oldest_mask = jnp.where(lax.broadcasted_iota(jnp.int32, (Q_BLOCK, Q_BLOCK), 1)
                            >= lax.broadcasted_iota(jnp.int32, (Q_BLOCK, Q_BLOCK), 0),
                            0.0, NEG_INF)
    n_q = ATT_TILE // Q_BLOCK
    first_near = N_KEY_BLOCKS - SB_NEAR_BLOCKS
    near_row0 = first_near * Q_BLOCK
    older = (WIN_TILES - 1) * ATT_TILE - near_row0

    def masked_queries(ref):
        q2 = ref[...] * (HEAD_DIM ** -0.5)
        return [jnp.where(_half_select(lane, half), q2, jnp.zeros_like(q2)) for half in range(2)]

    @pl.when(i == 0)
    def _():
        for half, qm in enumerate(masked_queries(q_ref)):
            znear_a[half, :, 0:older] = jnp.full((ATT_TILE, older), NEG_INF, F32)
            znear_a[half, :, older:] = _nt_dot(qm, k0_ref[...])

    def step(cur, nxt):
        knear[0:older, :] = k0_ref[ATT_TILE - older:, :]
        knear[older:, :] = kn_ref[...]
        for half, qm in enumerate(masked_queries(qn_ref)):
            nxt[half] = _nt_dot(qm, knear[...])

        _fill_window(vwin, (v2_ref, v1_ref, v0_ref))
        worst_of = _sb_near(cur, vwin, tri_ref, newest_mask, acc_s, later_s, *near_scratch)

        missing_rows = jnp.maximum(WIN_TILES - 1 - i, 0) * ATT_TILE
        worst = None
        for r in range(n_q):
            has_far_keys = (r + first_near) * Q_BLOCK > missing_rows
            worst_r = jnp.where(has_far_keys, worst_of[r], NEG_INF)
            worst = worst_r if worst is None else jnp.maximum(worst, worst_r)

        @pl.when(jnp.max(worst) >= EXIT_LOG_WEIGHT)
        def _():
            _fill_window(kwin, (k2_ref, k1_ref, k0_ref))
            for half, qm in enumerate(masked_queries(q_ref)):
                zfar[...] = _nt_dot(qm, kwin[0:(n_q - 1 + first_near) * Q_BLOCK, :])

                def _set(ncols):
                    zfar[:, 0:ncols] = jnp.full((ATT_TILE, ncols), NEG_INF, F32)
                _mask_missing_tiles(i, _set)

                for r in range(n_q):
                    rows = slice(r * Q_BLOCK, (r + 1) * Q_BLOCK)
                    z_of = lambda lo, hi, r=r, rows=rows: zfar[
                        rows, (r + lo) * Q_BLOCK:(r + hi) * Q_BLOCK]
                    w, _ = _sb_weights(z_of, _key_groups(0, first_near), later_s[half * n_q + r],
                                       tri_ref, newest_mask, oldest_mask)
                    acc_s[half, rows, :] += jnp.dot(
                        w, vwin[r * Q_BLOCK:(r + first_near) * Q_BLOCK, :],
                        preferred_element_type=F32)

        o_ref[...] = jnp.where(lane < HEAD_DIM, acc_s[0], acc_s[1]).astype(BF16)

    @pl.when(i % 2 == 0)
    def _():
        step(znear_a, znear_b)

    @pl.when(i % 2 == 1)
    def _():
        step(znear_b, znear_a)


def _sb_attention(p, tri, batch, seq):
    n = batch * seq
    n_tiles = seq // ATT_TILE
    n_q = ATT_TILE // Q_BLOCK
    near_row0 = (N_KEY_BLOCKS - SB_NEAR_BLOCKS) * Q_BLOCK
    far_rows = (n_q - 1 + N_KEY_BLOCKS - SB_NEAR_BLOCKS) * Q_BLOCK
    near_cols = SB_NEAR_BLOCKS * Q_BLOCK
    n_near_groups = len(_key_groups(N_KEY_BLOCKS - SB_NEAR_BLOCKS, N_KEY_BLOCKS))
    def tile(block, back):
        return pl.BlockSpec(
            (GROUP_LANES, ATT_TILE, LANE),
            lambda b, i: (block, b * n_tiles + jnp.clip(i - back, 0, n_tiles - 1), 0))

    near_rows = WIN_TILES * ATT_TILE - near_row0
    near_logits = pltpu.VMEM((GROUP_LANES, 2, ATT_TILE, near_rows), F32)
    window = list(range(WIN_TILES - 1, -1, -1))
    return pl.pallas_call(
        _sb_kernel,
        grid=(batch, n_tiles),
        in_specs=[tile(SB_Q, 0), tile(SB_Q, -1)]
        + [tile(SB_K, back) for back in window] + [tile(SB_K, -1)]
        + [tile(SB_V, back) for back in window]
        + [pl.BlockSpec(tri.shape, lambda b, i: (0, 0))],
        out_specs=tile(0, 0),
        out_shape=jax.ShapeDtypeStruct((GROUP_LANES, n, LANE), BF16),
        scratch_shapes=[pltpu.VMEM((WIN_TILES * ATT_TILE, LANE), BF16),
                        pltpu.VMEM((near_rows, LANE), BF16),
                        pltpu.VMEM((WIN_TILES * ATT_TILE, LANE), BF16),
                        near_logits, near_logits,
                        pltpu.VMEM((ATT_TILE, far_rows), F32),
                        pltpu.VMEM((2, ATT_TILE, LANE), F32),
                        pltpu.VMEM((2 * n_q, Q_BLOCK, LANE), F32),
                        pltpu.VMEM((2 * ATT_TILE, near_cols), BF16),
                        pltpu.VMEM((2 * ATT_TILE, near_cols), F32),
                        pltpu.VMEM((2 * ATT_TILE, near_cols), F32),
                        pltpu.VMEM((n_near_groups - 1, 2 * n_q, Q_BLOCK, LANE), F32),
                        pltpu.VMEM((n_q, 2 * Q_BLOCK, near_cols), BF16)],
        compiler_params=_params(("parallel", "arbitrary"), 56),
        name="sb_attention",
    )(*([p] * 9), tri)


def _bias_kernel(rb_ref, bucket_ref, band_ref, o_ref):
    bucket = bucket_ref[...]
    band = band_ref[...] > 0
    for h in range(o_ref.shape[0]):
        acc = jnp.zeros(bucket.shape, F32)
        for b in range(N_BUCKETS):
            acc = jnp.where(bucket == b, rb_ref[b, h], acc)
        o_ref[h] = jnp.where(band, acc, NEG_INF)


def _t5_bucket(rel):
    nb = N_BUCKETS // 2
    max_exact = nb // 2
    ret = jnp.where(rel > 0, nb, 0)
    n = jnp.abs(rel)
    large = max_exact + (jnp.log(jnp.maximum(n, 1).astype(jnp.float32) / max_exact)
                         / math.log(MAX_DISTANCE / max_exact) * (nb - max_exact)).astype(jnp.int32)
    large = jnp.minimum(large, nb - 1)
    return ret + jnp.where(n < max_exact, n, large)


def _diff_bias_table(rel_bias):
    heads = rel_bias.shape[1]
    a_key = jnp.arange(SPAN) - WINDOW
    a_qry = jnp.arange(Q_BLOCK)
    bucket = _t5_bucket(a_key[None, :] - a_qry[:, None]).astype(jnp.int32)
    kc = (a_key // CHUNK)[None, :]
    qc = (a_qry // CHUNK)[:, None]
    band = ((kc <= qc) & (kc >= qc - WINDOW // CHUNK)).astype(jnp.int32)
    return pl.pallas_call(
        _bias_kernel,
        in_specs=[pl.BlockSpec(memory_space=pltpu.SMEM),
                  pl.BlockSpec((Q_BLOCK, SPAN), lambda: (0, 0)),
                  pl.BlockSpec((Q_BLOCK, SPAN), lambda: (0, 0))],
        out_specs=pl.BlockSpec((heads, Q_BLOCK, SPAN), lambda: (0, 0, 0)),
        out_shape=jax.ShapeDtypeStruct((heads, Q_BLOCK, SPAN), F32),
        name="diff_bias_table",
    )(rel_bias.astype(F32), bucket, band)


def _diff_kernel(*refs, lam_init):
    operands, (bias_ref, lam_ref, g_ref, out_ref) = refs[:9], refs[9:13]
    kwin, vwin, logits_a, logits_b, lam_s = refs[13:]

    for h in range(DIFF_HEADS_PER_STEP):
        _diff_head(*[ref.at[h] for ref in operands], bias_ref.at[h], lam_ref, g_ref, out_ref.at[h],
                   kwin, vwin, logits_a.at[h], logits_b.at[h], lam_s, lam_init=lam_init)


def _diff_head(q_ref, qn_ref, k2_ref, k1_ref, k0_ref, kn_ref, v2_ref, v1_ref, v0_ref,
               bias_ref, lam_ref, g_ref, o_ref, kwin, vwin, logits_a, logits_b, lam_s, *, lam_init):
    i = pl.program_id(2)
    lane = lax.broadcasted_iota(jnp.int32, (1, LANE), 1)
    gain = g_ref[...] * (1.0 - lam_init)

    @pl.when(i == 0)
    def _():
        lv = lam_ref[...]
        lam0 = (jnp.exp(jnp.sum(lv[0:1] * lv[1:2], axis=-1, keepdims=True))
                - jnp.exp(jnp.sum(lv[2:3] * lv[3:4], axis=-1, keepdims=True)) + lam_init)
        lam_s[...] = jnp.broadcast_to(lam0, lam_s.shape)

    lam = lam_s[0:1, 0:1]

    def masked_queries(ref):
        q2 = ref[...] * (HEAD_DIM ** -0.5)
        return [jnp.where(_half_select(lane, m), q2, jnp.zeros_like(q2)) for m in range(2)]

    @pl.when(i == 0)
    def _():
        _fill_window(kwin, (k2_ref, k1_ref, k0_ref))
        ncols = (WIN_TILES - 1) * ATT_TILE
        for m, qm in enumerate(masked_queries(q_ref)):
            logits_a[m, :, 0:ncols] = jnp.full((ATT_TILE, ncols), NEG_INF, F32)
            logits_a[m, :, ncols:] = _nt_dot(qm, kwin[ncols:, :])

    def step(cur, nxt):
        _fill_window(kwin, (k1_ref, k0_ref, kn_ref))
        for m, qm in enumerate(masked_queries(qn_ref)):
            res = _nt_dot(qm, kwin[...])
            nxt[m, :, 0:ATT_TILE] = jnp.where(i == 0, NEG_INF, res[:, 0:ATT_TILE])
            nxt[m, :, ATT_TILE:] = res[:, ATT_TILE:]

        _fill_window(vwin, (v2_ref, v1_ref, v0_ref))
        bias = bias_ref[...]
        for r in range(ATT_TILE // Q_BLOCK):
            rows = slice(r * Q_BLOCK, (r + 1) * Q_BLOCK)
            probs = []
            for m in range(2):
                logits = cur[m, rows, r * Q_BLOCK:r * Q_BLOCK + SPAN] + bias
                e = jnp.exp(logits - jnp.max(logits, axis=-1, keepdims=True))
                probs.append((e, jnp.sum(e, axis=-1, keepdims=True)))
            attn = probs[0][0] * (1.0 / probs[0][1]) - probs[1][0] * (lam / probs[1][1])
            out = jnp.dot(attn.astype(BF16), vwin[r * Q_BLOCK:r * Q_BLOCK + SPAN, :],
                          preferred_element_type=F32)
            o_ref[rows, :] = _rms(out, gain).astype(BF16)

    @pl.when(i % 2 == 0)
    def _():
        step(logits_a, logits_b)

    @pl.when(i % 2 == 1)
    def _():
        step(logits_b, logits_a)


def _diff_attention(p, bias, lam_params, subln_gain, lam_init, batch, seq):
    n = batch * seq
    n_tiles = seq // ATT_TILE
    heads = bias.shape[0]
    per = DIFF_HEADS_PER_STEP
    assert heads % per == 0 and GROUP_LANES % per == 0

    def tile(block, back):
        return pl.BlockSpec(
            (per, ATT_TILE, LANE),
            lambda b, g, i: (block * (GROUP_LANES // per) + g,
                             b * n_tiles + jnp.clip(i - back, 0, n_tiles - 1), 0))

    window = list(range(WIN_TILES - 1, -1, -1))
    logits = pltpu.VMEM((per, 2, ATT_TILE, WIN_TILES * ATT_TILE), F32)
    return pl.pallas_call(
        functools.partial(_diff_kernel, lam_init=lam_init),
        grid=(batch, heads // per, n_tiles),
        in_specs=[tile(DF_Q, 0), tile(DF_Q, -1)]
        + [tile(DF_K, back) for back in window] + [tile(DF_K, -1)]
        + [tile(DF_V, back) for back in window]
        + [pl.BlockSpec((per, Q_BLOCK, SPAN), lambda b, g, i: (g, 0, 0)),
           pl.BlockSpec(lam_params.shape, lambda b, g, i: (0, 0)),
           pl.BlockSpec((1, LANE), lambda b, g, i: (0, 0))],
        out_specs=tile(0, 0),
        out_shape=jax.ShapeDtypeStruct((heads, n, LANE), BF16),
        scratch_shapes=[pltpu.VMEM((WIN_TILES * ATT_TILE, LANE), BF16),
                        pltpu.VMEM((WIN_TILES * ATT_TILE, LANE), BF16),
                        logits, logits, pltpu.VMEM((8, LANE), F32)],
        compiler_params=_params(("parallel", "parallel", "arbitrary"), 48),
        name="diff_attention",
    )(*([p] * 9), bias, lam_params, subln_gain)


def _softplus(v):
    return jnp.maximum(v, 0.0) + jnp.log1p(jnp.exp(-jnp.abs(v)))


def _expm1(y):
    u = jnp.exp(y)
    is_one = u == 1.0
    near_zero = jnp.where(is_one, y, (u - 1.0) * y / jnp.where(is_one, 1.0, jnp.log(u)))
    return jnp.where(y > -0.5, near_zero, u - 1.0)


def _causal_conv(padded, taps):
    n_taps = taps.shape[0]
    rows, width = padded.shape
    groups = padded.reshape(rows // 8, 8, width)
    sub = lax.broadcasted_iota(jnp.int32, (1, 8, width), 1)
    out = groups[1:] * taps[n_taps - 1:n_taps]
    for k in range(n_taps - 1):
        back = n_taps - 1 - k
        turned = pltpu.roll(groups, back, 1)
        out = out + jnp.where(sub >= back, turned[1:], turned[:-1]) * taps[k:k + 1]
    return out.reshape(rows - 8, width)


def _recurrent_kernel(x_ref, gate_ref, scb_ref, scc_ref, scx_ref,
                      cw_ref, cb_ref, wg_ref, bg_ref, lam_ref, scw_ref,
                      ylru_ref, ysc_ref,
                      xext, sext, a_s, b_s, h_s, carry):
    t = x_ref.shape[1]

    @pl.when(pl.program_id(1) == 0)
    def _():
        xext[0:HALO, :] = jnp.zeros((HALO, GROUP_W), F32)
        sext[0:HALO, :] = jnp.zeros((HALO, GROUP_W), F32)
        carry[...] = jnp.zeros(carry.shape, F32)

    for c in range(GROUP_LANES):
        lanes = slice(c * LANE, (c + 1) * LANE)
        xext[HALO:, lanes] = x_ref[c].astype(F32)
        sext[HALO:, lanes] = scc_ref[c].astype(F32) * scx_ref[c].astype(F32)

    for c in range(GROUP_LANES):
        lanes = slice(c * LANE, (c + 1) * LANE)
        xc = cb_ref[:, lanes] + _causal_conv(xext[HALO - 8:, lanes], cw_ref[:, lanes])
        gates = jax.nn.sigmoid(
            jnp.dot(xc.astype(BF16), wg_ref[c], preferred_element_type=F32) + bg_ref[c])
        r_gate = gates[:, :LANE]
        i_gate = gates[:, LANE:]
        log_a = (-LRU_C) * r_gate * _softplus(-lam_ref[:, lanes])
        a_s[:, lanes] = jnp.exp(log_a)
        b_s[:, lanes] = jnp.sqrt(-_expm1(2.0 * log_a)) * (i_gate * xc)

        sc = _causal_conv(sext[HALO - 8:, lanes], scw_ref[:, lanes])
        ysc_ref[c] = (scb_ref[c].astype(F32) * sc).astype(BF16)

    xext[0:HALO, :] = xext[t:t + HALO, :]
    sext[0:HALO, :] = sext[t:t + HALO, :]

    sub = lax.broadcasted_iota(jnp.int32, (8, GROUP_W), 0)

    def step(j, h_prev):
        rows = pl.ds(pl.multiple_of(j * 8, 8), 8)
        a = a_s[rows, :]
        b = b_s[rows, :]
        for d in (1, 2, 4):
            keep = sub >= d
            b = a * jnp.where(keep, pltpu.roll(b, d, 0), 0.0) + b
            a = a * jnp.where(keep, pltpu.roll(a, d, 0), 1.0)
        h = a * h_prev + b
        h_s[rows, :] = h
        return jnp.broadcast_to(h[7:8, :], (8, GROUP_W))

    carry[...] = lax.fori_loop(0, t // 8, step, carry[...], unroll=4)

    for c in range(GROUP_LANES):
        lanes = slice(c * LANE, (c + 1) * LANE)
        ylru_ref[c] = (h_s[:, lanes]
                       * jax.nn.gelu(gate_ref[c].astype(F32), approximate=True)).astype(BF16)


def _recurrent(p, conv_w, conv_b, w_gate, b_gate, lam, sc_w, batch, seq, t=512):
    n = batch * seq
    n_tiles = seq // t
    blk = lambda which: pl.BlockSpec((GROUP_LANES, t, LANE),
                                     lambda b, i: (which, b * n_tiles + i, 0))
    full = lambda a: pl.BlockSpec(a.shape, lambda b, i: (0,) * a.ndim)
    small = (conv_w, conv_b, w_gate, b_gate, lam, sc_w)
    out = jax.ShapeDtypeStruct((GROUP_LANES, n, LANE), BF16)
    return pl.pallas_call(
        _recurrent_kernel,
        grid=(batch, n_tiles),
        in_specs=[blk(LRU_X), blk(LRU_G), blk(SC_B), blk(SC_C), blk(SC_X)] + [full(a) for a in small],
        out_specs=[blk(0), blk(0)],
        out_shape=[out, out],
        scratch_shapes=[pltpu.VMEM((t + HALO, GROUP_W), F32),
                        pltpu.VMEM((t + HALO, GROUP_W), F32),
                        pltpu.VMEM((t, GROUP_W), F32),
                        pltpu.VMEM((t, GROUP_W), F32),
                        pltpu.VMEM((t, GROUP_W), F32),
                        pltpu.VMEM((8, GROUP_W), F32)],
        compiler_params=_params(("parallel", "arbitrary"), 32),
        name="recurrent",
    )(p, p, p, p, p, *small)


def _gate_weights(w_gate, b_gate):
    per_group = LANE // w_gate.shape[-1]
    zeros = jnp.zeros(w_gate.shape[-2:], w_gate.dtype)
    groups = []
    for c in range(GROUP_LANES):
        halves = []
        for gate in range(2):
            blocks = [w_gate[gate, c * per_group + j] for j in range(per_group)]
            halves.append(jnp.block([[blocks[a] if a == b else zeros for b in range(per_group)]
                                     for a in range(per_group)]))
        groups.append(jnp.concatenate(halves, axis=1))
    wg = jnp.stack(groups).astype(BF16)
    bg = jnp.concatenate([b_gate[0].reshape(GROUP_LANES, 1, LANE),
                          b_gate[1].reshape(GROUP_LANES, 1, LANE)], axis=-1).astype(F32)
    return wg, bg


def _outproj_kernel(sb_ref, df_ref, lru_ref, sc_ref, w_ref, x_ref, g1_ref, g2_ref,
                    xo_ref, h_ref, mixed):
    for a, ref in enumerate((sb_ref, df_ref, lru_ref, sc_ref)):
        for c in range(GROUP_LANES):
            col = (a * GROUP_LANES + c) * LANE
            mixed[:, col:col + LANE] = ref[c]
    y = jnp.dot(mixed[...], w_ref[...], preferred_element_type=F32)
    x_new = x_ref[...] + _rms(y, g1_ref[...])
    xo_ref[...] = x_new
    h_ref[...] = _rms(x_new, g2_ref[...]).astype(BF16)


def _outproj(y_sb, y_df, y_lru, y_sc, w, x2, g1, g2, tm=512):
    n, d = x2.shape
    blk = pl.BlockSpec((GROUP_LANES, tm, LANE), lambda i: (0, i, 0))
    row = pl.BlockSpec((tm, d), lambda i: (i, 0))
    vec = pl.BlockSpec((1, d), lambda i: (0, 0))
    return pl.pallas_call(
        _outproj_kernel,
        grid=(n // tm,),
        in_specs=[blk, blk, blk, blk, pl.BlockSpec(w.shape, lambda i: (0, 0)), row, vec, vec],
        out_specs=[row, row],
        out_shape=[jax.ShapeDtypeStruct((n, d), F32), jax.ShapeDtypeStruct((n, d), BF16)],
        scratch_shapes=[pltpu.VMEM((tm, w.shape[0]), BF16)],
        compiler_params=_params(("parallel",), 48),
        name="outproj",
    )(y_sb, y_df, y_lru, y_sc, w, x2, g1, g2)


def _ffn_up_kernel(h_ref, wg_ref, wu_ref, cg_ref, cu_ref, o_ref, ug_s, uu_s, carry_g, carry_u,
                   *, tiles_per_seq):
    i = pl.program_id(0)
    j = pl.program_id(1)
    tm, tf = o_ref.shape
    @pl.when(i == 0)
    def _():
        carry_g[j] = jnp.zeros(carry_g.shape[1:], F32)
        carry_u[j] = jnp.zeros(carry_u.shape[1:], F32)

    first = (i % tiles_per_seq) == 0
    ug_s[0:8, :] = jnp.where(first, 0.0, carry_g[j])
    uu_s[0:8, :] = jnp.where(first, 0.0, carry_u[j])

    n_blocks = tm // FFN_ROWS

    def project(c, k):
        cols = slice(c * MXU_COLS, (c + 1) * MXU_COLS)
        rb = k % n_blocks
        w_ref, dst_ref = ((wg_ref, ug_s), (wu_ref, uu_s))[k // n_blocks]
        dst_ref[8 + rb * FFN_ROWS:8 + (rb + 1) * FFN_ROWS, cols] = jnp.dot(
            h_ref[rb * FFN_ROWS:(rb + 1) * FFN_ROWS, :], w_ref[:, cols], preferred_element_type=F32)

    strips_per_dot = tm // (2 * n_blocks * FFN_STRIP)

    def activate(c, k):
        cols = slice(c * MXU_COLS, (c + 1) * MXU_COLS)
        src = slice(k * FFN_STRIP, (k + 1) * FFN_STRIP + 8)
        gate = _causal_conv(ug_s[src, cols], cg_ref[:, cols])
        up = _causal_conv(uu_s[src, cols], cu_ref[:, cols])
        o_ref[k * FFN_STRIP:(k + 1) * FFN_STRIP, cols] = (
            jax.nn.gelu(gate, approximate=True) * up).astype(BF16)

    n_chunks = tf // MXU_COLS
    for c in range(n_chunks + 1):
        for k in range(2 * n_blocks):
            if c < n_chunks:
                project(c, k)
            if c > 0:
                for s in range(k * strips_per_dot, (k + 1) * strips_per_dot):
                    activate(c - 1, s)
    carry_g[j] = ug_s[tm:tm + 8, :]
    carry_u[j] = uu_s[tm:tm + 8, :]


def _ffn_up(h2, w_up, conv_w, seq, tm=1024, tf=1024):
    n, d = h2.shape
    d_ff = w_up.shape[1] // 2
    n_f = d_ff // tf
    assert seq % tm == 0 and d_ff % tf == 0
    return pl.pallas_call(
        functools.partial(_ffn_up_kernel, tiles_per_seq=seq // tm),
        grid=(n // tm, n_f),
        in_specs=[
            pl.BlockSpec((tm, d), lambda i, j: (i, 0)),
            pl.BlockSpec((d, tf), lambda i, j: (0, j)),
            pl.BlockSpec((d, tf), lambda i, j: (0, n_f + j)),
            pl.BlockSpec((FFN_CONV, tf), lambda i, j: (0, j)),
            pl.BlockSpec((FFN_CONV, tf), lambda i, j: (0, n_f + j)),
        ],
        out_specs=pl.BlockSpec((tm, tf), lambda i, j: (i, j)),
        out_shape=jax.ShapeDtypeStruct((n, d_ff), BF16),
        scratch_shapes=[pltpu.VMEM((8 + tm, tf), F32), pltpu.VMEM((8 + tm, tf), F32),
                        pltpu.VMEM((n_f, 8, tf), F32), pltpu.VMEM((n_f, 8, tf), F32)],
        compiler_params=_params(("arbitrary", "arbitrary"), 48),
        name="ffn_up",
    )(h2, w_up, w_up, conv_w, conv_w)


def _ffn_down_kernel(a_ref, w_ref, x_ref, g_ref, o_ref):
    y = jnp.dot(a_ref[...], w_ref[...], preferred_element_type=F32)
    o_ref[...] = x_ref[...] + _rms(y, g_ref[...])


def _ffn_down(act, w_down, x2, gain, tm=512):
    n, d = x2.shape
    row = pl.BlockSpec((tm, d), lambda i: (i, 0))
    return pl.pallas_call(
        _ffn_down_kernel,
        grid=(n // tm,),
        in_specs=[
            pl.BlockSpec((tm, act.shape[1]), lambda i: (i, 0)),
            pl.BlockSpec(w_down.shape, lambda i: (0, 0), pipeline_mode=pl.Buffered(1)),
            row,
            pl.BlockSpec((1, d), lambda i: (0, 0)),
        ],
        out_specs=row,
        out_shape=jax.ShapeDtypeStruct((n, d), F32),
        compiler_params=_params(("parallel",), 48),
        name="ffn_down",
    )(act, w_down, x2, gain)


def _later_matrix():
    idx = np.arange(SB_GROUP * Q_BLOCK)
    return jnp.asarray(idx[:, None] > idx[None, :], BF16)


def kernel(x, norm_gains, w_in, w_out, rel_bias, diff_lambda, diff_subln_g, lru_conv_w, lru_conv_b,
           lru_w_gate, lru_b_gate, lru_lambda, sc_conv_w, ffn_w_up, ffn_conv_w, ffn_w_down):
    batch, seq, d = x.shape
    depth = w_in.shape[0]
    assert seq % ATT_TILE == 0 and w_in.shape[2] == N_IN_BLOCKS * GROUP_W
    x2 = x.reshape(batch * seq, d).astype(F32)
    tri = _later_matrix()
    bias = _diff_bias_table(rel_bias)
    row = lambda v: v.reshape(1, -1).astype(F32)
    for l in range(depth):
        lam_init = 0.8 - 0.6 * math.exp(-0.3 * l)
        p = _inproj(x2, row(norm_gains[l, 0]), _layer_bf16(w_in, l))
        y_sb = _sb_attention(p, tri, batch, seq)
        y_df = _diff_attention(p, bias, diff_lambda[l].astype(F32), row(diff_subln_g[l]),
                               lam_init, batch, seq)
        wg, bg = _gate_weights(lru_w_gate[l], lru_b_gate[l])
        y_lru, y_sc = _recurrent(p, lru_conv_w[l].astype(F32), row(lru_conv_b[l]), wg, bg,
                                 row(lru_lambda[l]), sc_conv_w[l].astype(F32), batch, seq)
        x2, h2 = _outproj(y_sb, y_df, y_lru, y_sc, _layer_bf16(w_out, l), x2,
                          row(norm_gains[l, 1]), row(norm_gains[l, 2]))
        act = _ffn_up(h2, _layer_bf16(ffn_w_up, l), ffn_conv_w[l].astype(F32), seq)
        x2 = _ffn_down(act, _layer_bf16(ffn_w_down, l), x2, row(norm_gains[l, 3]))
    return x2.reshape(batch, seq, d).astype(x.dtype)
```
